```python
import jax, jax.numpy as jnp
from jax import lax
import numpy as np

D_MODEL = 1024
BATCH = 32
SEQ = 2048
DEPTH = 4

N_EVEN = (DEPTH + 1) // 2
N_ODD = DEPTH // 2

ATT_HEADS = 8
ATT_KV_HEADS = 2
ATT_HEAD_DIM = 64
WINDOW = 128
ROPE_THETA = 10000.0
ATT_WIDTH = ATT_HEADS * ATT_HEAD_DIM
KV_WIDTH = ATT_KV_HEADS * ATT_HEAD_DIM
POOL_WINDOWS = (2, 4, 8, 16)
POOL_GROUP = 128
POOL_WIDTH = POOL_GROUP * len(POOL_WINDOWS)
AB_IN = ATT_WIDTH + 2 * KV_WIDTH + POOL_WIDTH
AB_OUT = ATT_WIDTH + POOL_WIDTH
HGRN_EXPAND = 128
HGRN_HEADS = D_MODEL // HGRN_EXPAND
HGRN_DK = HGRN_EXPAND
HGRN_DV = D_MODEL // HGRN_HEADS
HGRN_KWIDTH = HGRN_HEADS * HGRN_DK
HGRN_VWIDTH = HGRN_HEADS * HGRN_DV
HGRN_IN = 2 * HGRN_KWIDTH + 2 * HGRN_VWIDTH
HGRN_CHUNK = 32
D_FF = 2816
NORM_EPS = 1e-6
GATE_EPS = 1e-6

kernel_name = "hybrid_swa_pool_hgrn2_macaron"


def rms_norm(x, gain):
    xf = x.astype(jnp.float32)
    y = xf * lax.rsqrt(jnp.mean(xf * xf, axis=-1, keepdims=True) + NORM_EPS)
    return (y * gain.astype(jnp.float32)).astype(x.dtype)


def swiglu(x, w_gate, w_up, w_down):
    return (jax.nn.silu(x @ w_gate) * (x @ w_up)) @ w_down


def rope(x, positions):
    half = x.shape[-1] // 2
    inv_freq = ROPE_THETA ** (-jnp.arange(half, dtype=jnp.float32) / half)
    ang = positions.astype(jnp.float32)[:, None] * inv_freq[None, :]
    cos = jnp.cos(ang)[None, :, None, :]
    sin = jnp.sin(ang)[None, :, None, :]
    xf = x.astype(jnp.float32)
    x1, x2 = xf[..., :half], xf[..., half:]
    return jnp.concatenate([x1 * cos - x2 * sin, x2 * cos + x1 * sin], axis=-1).astype(x.dtype)


def sliding_window_attention(q, k, v, sinks):
    B, T, H, d = q.shape
    nb = T // WINDOW
    G = H // ATT_KV_HEADS
    qb = q.reshape(B, nb, WINDOW, ATT_KV_HEADS, G, d).transpose(1, 0, 2, 3, 4, 5)

    def band_keys(a):
        ab = a.reshape(B, nb, WINDOW, ATT_KV_HEADS, d)
        prev = jnp.pad(ab, ((0, 0), (1, 0), (0, 0), (0, 0), (0, 0)))[:, :nb]
        return jnp.concatenate([prev, ab], axis=2).transpose(1, 0, 2, 3, 4)

    k_win, v_win = band_keys(k), band_keys(v)
    r = jnp.arange(WINDOW)[:, None]
    c = jnp.arange(2 * WINDOW)[None, :]
    rel = WINDOW + r - c
    band = (rel >= 0) & (rel < WINDOW)
    sink = sinks.astype(jnp.float32).reshape(ATT_KV_HEADS, G)[None, :, :, None, None]
    scale = d ** -0.5

    def block(args):
        qi, ki, vi, idx = args
        s = jnp.einsum('bqkgd,bskd->bkgqs', qi.astype(jnp.float32), ki.astype(jnp.float32)) * scale
        mask = band & ((c >= WINDOW) | (idx > 0))
        s = jnp.where(mask, s, -jnp.inf)
        m = jnp.maximum(jnp.max(s, axis=-1, keepdims=True), sink)
        p = jnp.where(mask, jnp.exp(s - m), 0.0)
        denom = jnp.sum(p, axis=-1, keepdims=True) + jnp.exp(sink - m)
        o = jnp.einsum('bkgqs,bskd->bqkgd', p / denom, vi.astype(jnp.float32))
        return o.astype(qi.dtype)

    out = lax.map(block, (qb, k_win, v_win, jnp.arange(nb)))
    return out.transpose(1, 0, 2, 3, 4, 5).reshape(B, T, H * d)


def multiscale_pool(u, w_pool, pool_scale):
    T = u.shape[1]
    uf = u.astype(jnp.float32)
    cs = jnp.cumsum(uf, axis=1)
    count = jnp.arange(1, T + 1, dtype=jnp.float32)[None, :, None]
    outs = []
    for gi, w in enumerate(POOL_WINDOWS):
        sl = slice(gi * POOL_GROUP, (gi + 1) * POOL_GROUP)
        cg = cs[..., sl]
        lag = jnp.pad(cg, ((0, 0), (w, 0), (0, 0)))[:, :T]
        mean = (cg - lag) / jnp.minimum(count, float(w))
        outs.append((mean - uf[..., sl]).astype(u.dtype) @ w_pool[gi])
    return jnp.concatenate(outs, axis=-1) * pool_scale


def hgrn2_chunkwise(q, f_logit, v, lb):
    B, T, H, dk = q.shape
    dv = v.shape[-1]
    C = HGRN_CHUNK
    nc = T // C
    lbf = lb.astype(jnp.float32).reshape(H, dk)
    z = f_logit.astype(jnp.float32)
    sig = jax.nn.sigmoid(z)
    f = lbf + (1.0 - lbf) * sig
    log_f = jnp.log(jnp.maximum(f, GATE_EPS))
    key = 1.0 - f
    qf = jax.nn.silu(q.astype(jnp.float32))

    def to_chunks(a):
        return a.reshape(B, nc, C, H, a.shape[-1]).swapaxes(0, 1)

    causal = jnp.tril(jnp.ones((C, C), dtype=bool))[None, :, :, None, None]

    def step(S, inp):
        qc, kc, gc, vc = inp
        Gc = jnp.cumsum(gc, axis=1)
        o_inter = jnp.einsum('bchk,bhkv->bchv', qc * jnp.exp(Gc), S)
        diff = jnp.where(causal, Gc[:, :, None] - Gc[:, None, :], 0.0)
        decay = jnp.where(causal, jnp.exp(diff), 0.0)
        scores = jnp.einsum('bthk,btshk,bshk->bhts', qc, decay, kc)
        o_intra = jnp.einsum('bhts,bshv->bthv', scores, vc)
        G_last = Gc[:, -1]
        S = jnp.exp(G_last)[..., None] * S + jnp.einsum(
            'bshk,bshv->bhkv', kc * jnp.exp(G_last[:, None] - Gc), vc)
        return S, o_inter + o_intra

    S0 = jnp.zeros((B, H, dk, dv), jnp.float32)
    _, o = lax.scan(step, S0, (to_chunks(qf), to_chunks(key), to_chunks(log_f),
                               to_chunks(v.astype(jnp.float32))))
    return o.swapaxes(0, 1).reshape(B, T, H, dv)


def attn_pool_mixer(h, positions, w_in, w_out, q_gain, k_gain, sinks, w_pool, pool_scale):
    B, T, _ = h.shape
    proj = h @ w_in
    q, k, v, u = jnp.split(proj, [ATT_WIDTH, ATT_WIDTH + KV_WIDTH, ATT_WIDTH + 2 * KV_WIDTH], axis=-1)
    q = rope(rms_norm(q.reshape(B, T, ATT_HEADS, ATT_HEAD_DIM), q_gain), positions)
    k = rope(rms_norm(k.reshape(B, T, ATT_KV_HEADS, ATT_HEAD_DIM), k_gain), positions)
    v = v.reshape(B, T, ATT_KV_HEADS, ATT_HEAD_DIM)
    a = sliding_window_attention(q, k, v, sinks)
    p = multiscale_pool(u, w_pool, pool_scale)
    return jnp.concatenate([a, p], axis=-1) @ w_out


def hgrn_mixer(h, w_in, w_out, out_gain, lb):
    B, T, _ = h.shape
    proj = h @ w_in
    q, f, i, g = jnp.split(proj, [HGRN_KWIDTH, 2 * HGRN_KWIDTH, 2 * HGRN_KWIDTH + HGRN_VWIDTH], axis=-1)
    o = hgrn2_chunkwise(q.reshape(B, T, HGRN_HEADS, HGRN_DK), f.reshape(B, T, HGRN_HEADS, HGRN_DK),
                        i.reshape(B, T, HGRN_HEADS, HGRN_DV), lb)
    gate = jax.nn.sigmoid(g.reshape(B, T, HGRN_HEADS, HGRN_DV).astype(jnp.float32))
    o = rms_norm(o * gate, out_gain).astype(h.dtype).reshape(B, T, HGRN_VWIDTH)
    return o @ w_out


def setup_inputs(seed: int = 0) -> dict:
    key = jax.random.key(seed)
    ks = jax.random.split(key, 16)
    f32 = jnp.float32

    def w(k, shape, fan_in):
        return jax.random.normal(k, shape, f32) * fan_in ** -0.5

    x = jax.random.normal(ks[0], (BATCH, SEQ, D_MODEL), f32)
    positions = jnp.arange(SEQ, dtype=jnp.int32)
    norm_gains = 1.0 + 0.1 * jax.random.normal(ks[1], (DEPTH, 3, D_MODEL), f32)
    ffn_w_gate = w(ks[2], (DEPTH, 2, D_MODEL, D_FF), D_MODEL)
    ffn_w_up = w(ks[3], (DEPTH, 2, D_MODEL, D_FF), D_MODEL)
    ffn_w_down = w(ks[4], (DEPTH, 2, D_FF, D_MODEL), D_FF)
    ab_w_in = w(ks[5], (N_EVEN, D_MODEL, AB_IN), D_MODEL)
    ab_w_out = w(ks[6], (N_EVEN, AB_OUT, D_MODEL), AB_OUT)
    q_norm_gain = 1.0 + 0.1 * jax.random.normal(ks[7], (N_EVEN, ATT_HEAD_DIM), f32)
    k_norm_gain = 1.0 + 0.1 * jax.random.normal(ks[8], (N_EVEN, ATT_HEAD_DIM), f32)
    attn_sinks = 0.5 * jax.random.normal(ks[9], (N_EVEN, ATT_HEADS), f32)
    pool_w = w(ks[10], (N_EVEN, len(POOL_WINDOWS), POOL_GROUP, POOL_GROUP), POOL_GROUP)
    pool_scale = 1.0 + 0.1 * jax.random.normal(ks[11], (N_EVEN, POOL_WIDTH), f32)
    c_w_in = w(ks[12], (N_ODD, D_MODEL, HGRN_IN), D_MODEL)
    c_w_out = w(ks[13], (N_ODD, HGRN_VWIDTH, D_MODEL), HGRN_VWIDTH)
    c_out_norm_gain = 1.0 + 0.1 * jax.random.normal(ks[14], (N_ODD, HGRN_DV), f32)
    lb_logits = jax.random.normal(ks[15], (N_ODD, HGRN_KWIDTH), f32)
    return {"x": x, "positions": positions, "norm_gains": norm_gains,
            "ffn_w_gate": ffn_w_gate, "ffn_w_up": ffn_w_up, "ffn_w_down": ffn_w_down,
            "ab_w_in": ab_w_in, "ab_w_out": ab_w_out, "q_norm_gain": q_norm_gain,
            "k_norm_gain": k_norm_gain, "attn_sinks": attn_sinks, "pool_w": pool_w,
            "pool_scale": pool_scale, "c_w_in": c_w_in, "c_w_out": c_w_out,
            "c_out_norm_gain": c_out_norm_gain, "lb_logits": lb_logits}


def reference(x, positions, norm_gains, ffn_w_gate, ffn_w_up, ffn_w_down,
              ab_w_in, ab_w_out, q_norm_gain, k_norm_gain, attn_sinks, pool_w,
              pool_scale, c_w_in, c_w_out, c_out_norm_gain, lb_logits):
    P = jax.nn.softmax(lb_logits.astype(jnp.float32), axis=0)
    lower_bounds = jnp.cumsum(P, axis=0) - P[0]
    for layer in range(DEPTH):
        h = rms_norm(x, norm_gains[layer, 0])
        x = x + 0.5 * swiglu(h, ffn_w_gate[layer, 0], ffn_w_up[layer, 0], ffn_w_down[layer, 0])
        h = rms_norm(x, norm_gains[layer, 1])
        j = layer // 2
        if layer % 2 == 0:
            x = x + attn_pool_mixer(h, positions, ab_w_in[j], ab_w_out[j], q_norm_gain[j],
                                    k_norm_gain[j], attn_sinks[j], pool_w[j], pool_scale[j])
        else:
            x = x + hgrn_mixer(h, c_w_in[j], c_w_out[j], c_out_norm_gain[j], lower_bounds[j])
        h = rms_norm(x, norm_gains[layer, 2])
        x = x + 0.5 * swiglu(h, ffn_w_gate[layer, 1], ffn_w_up[layer, 1], ffn_w_down[layer, 1])
    return x
```

```python
import functools

import jax
import jax.numpy as jnp
import numpy as np
from jax import lax
from jax.experimental import pallas as pl
from jax.experimental.pallas import tpu as pltpu

F32 = jnp.float32
BF16 = jnp.bfloat16

NORM_EPS = 1e-6
GATE_EPS = 1e-6
ROPE_THETA = 10000.0

ATT_HEADS = 8
ATT_KV_HEADS = 2
ATT_GROUP = ATT_HEADS // ATT_KV_HEADS
ATT_HEAD_DIM = 64
WINDOW = 128
ATT_WIDTH = ATT_HEADS * ATT_HEAD_DIM
KV_WIDTH = ATT_KV_HEADS * ATT_HEAD_DIM
POOL_WINDOWS = (2, 4, 8, 16)
POOL_GROUP = 128
POOL_WIDTH = POOL_GROUP * len(POOL_WINDOWS)
POOL_HISTORY = 16
HGRN_HEADS = 8
HGRN_DK = 128

FFN_ROW_TILE = 512
FFN_COL_CHUNK = 256
AB_ROW_TILE = 512
HGRN_ROW_TILE = 256
HGRN_CHUNK = 128
HGRN_BASE_BLOCK = 8
VMEM_LIMIT_BYTES = 56 * 1024 * 1024
MASKED_SCORE = -1e30


def _rms_norm_rows(x, gain):
    ms = jnp.mean(x * x, axis=-1, keepdims=True)
    return x * lax.rsqrt(ms + NORM_EPS) * gain


def _const_spec(shape):
    nd = len(shape)
    return pl.BlockSpec(shape, lambda *_: (0,) * nd, pipeline_mode=pl.Buffered(1))


def _dot(a, b):
    return jnp.dot(a, b, preferred_element_type=F32)


def _dot_nt(a, b):
    return lax.dot_general(a, b, (((1,), (1,)), ((), ())), preferred_element_type=F32)


def _dot_tn(a, b):
    return lax.dot_general(a, b, (((0,), (0,)), ((), ())), preferred_element_type=F32)


def _ffn_kernel(x_ref, gain_ref, wg_ref, wu_ref, wd_ref, o_ref):
    x = x_ref[...]
    h = _rms_norm_rows(x, gain_ref[...]).astype(BF16)
    d_ff = wg_ref.shape[1]
    acc = None
    for c in range(d_ff // FFN_COL_CHUNK):
        sl = slice(c * FFN_COL_CHUNK, (c + 1) * FFN_COL_CHUNK)
        g = _dot(h, wg_ref[:, sl])
        u = _dot(h, wu_ref[:, sl])
        a = (g * jax.nn.sigmoid(g) * u).astype(BF16)
        d = _dot(a, wd_ref[sl, :])
        acc = d if acc is None else acc + d
    o_ref[...] = x + 0.5 * acc


def _ffn(x2, gain, wg, wu, wd):
    n, d = x2.shape
    d_ff = wg.shape[1]
    tm = min(FFN_ROW_TILE, n)
    return pl.pallas_call(
        _ffn_kernel,
        out_shape=jax.ShapeDtypeStruct((n, d), F32),
        grid=(n // tm,),
        in_specs=[
            pl.BlockSpec((tm, d), lambda i: (i, 0)),
            _const_spec((1, d)),
            _const_spec((d, d_ff)),
            _const_spec((d, d_ff)),
            _const_spec((d_ff, d)),
        ],
        out_specs=pl.BlockSpec((tm, d), lambda i: (i, 0)),
        compiler_params=pltpu.CompilerParams(
            dimension_semantics=("arbitrary",), vmem_limit_bytes=VMEM_LIMIT_BYTES),
        name="ffn",
    )(x2, gain.reshape(1, d), wg, wu, wd)


def _rope_table_kernel(pos_ref, inv_freq_ref, sign_ref, cos_ref, sin_ref):
    ang = pos_ref[...].astype(F32) * inv_freq_ref[...]
    cos_ref[...] = jnp.cos(ang)
    sin_ref[...] = jnp.sin(ang) * sign_ref[...]


def _rope_tables(positions):
    t = positions.shape[0]
    half = ATT_HEAD_DIM // 2
    lane = np.arange(ATT_WIDTH)
    inv_freq = ROPE_THETA ** (-jnp.arange(half, dtype=F32) / half)
    inv_freq_lanes = jnp.tile(inv_freq, ATT_WIDTH // half).reshape(1, ATT_WIDTH)
    sign = jnp.asarray(np.where(lane % ATT_HEAD_DIM < half, -1.0, 1.0), F32).reshape(1, ATT_WIDTH)
    tr = min(256, t)
    return pl.pallas_call(
        _rope_table_kernel,
        out_shape=(jax.ShapeDtypeStruct((t, ATT_WIDTH), F32),) * 2,
        grid=(t // tr,),
        in_specs=[pl.BlockSpec((tr, 1), lambda i: (i, 0)),
                  _const_spec((1, ATT_WIDTH)), _const_spec((1, ATT_WIDTH))],
        out_specs=(pl.BlockSpec((tr, ATT_WIDTH), lambda i: (i, 0)),) * 2,
        compiler_params=pltpu.CompilerParams(dimension_semantics=("arbitrary",)),
        name="rope_tables",
    )(positions.reshape(t, 1), inv_freq_lanes, sign)


def _rotate_half_partner(z):
    w = z.shape[1]
    lane = lax.broadcasted_iota(jnp.int32, z.shape, 1)
    first_half = (lane & (ATT_HEAD_DIM // 2)) == 0
    return jnp.where(first_half, pltpu.roll(z, w - ATT_HEAD_DIM // 2, axis=1),
                     pltpu.roll(z, ATT_HEAD_DIM // 2, axis=1))


def _ab_kernel(sink_ref, x_ref, gain_ref, w_in_ref, cos_ref, sin_ref, qg_ref, kg_ref, bd_ref,
               wpool_ref, pscale_ref, w_out_ref, o_ref,
               kprev_ref, vprev_ref, uhist_ref, cat_ref):
    tm = x_ref.shape[0]
    j = pl.program_id(1)

    @pl.when(j == 0)
    def _():
        kprev_ref[...] = jnp.zeros_like(kprev_ref)
        vprev_ref[...] = jnp.zeros_like(vprev_ref)
        uhist_ref[...] = jnp.zeros_like(uhist_ref)

    x = x_ref[...]
    h = _rms_norm_rows(x, gain_ref[...]).astype(BF16)
    proj = _dot(h, w_in_ref[...])
    q = proj[:, :ATT_WIDTH]
    k = proj[:, ATT_WIDTH:ATT_WIDTH + KV_WIDTH]
    v = proj[:, ATT_WIDTH + KV_WIDTH:ATT_WIDTH + 2 * KV_WIDTH]
    u = proj[:, ATT_WIDTH + 2 * KV_WIDTH:]

    bd = bd_ref[...]
    q_ms = _dot((q * q).astype(BF16), bd)
    k_ms = _dot((k * k).astype(BF16), bd[:KV_WIDTH, :KV_WIDTH])
    cos = cos_ref[...]
    sin = sin_ref[...]
    qn = q * lax.rsqrt(q_ms + NORM_EPS) * qg_ref[...]
    kn = k * lax.rsqrt(k_ms + NORM_EPS) * kg_ref[...]
    scale = ATT_HEAD_DIM ** -0.5
    qr = ((qn * cos + _rotate_half_partner(qn) * sin) * scale).astype(BF16)
    kr = (kn * cos[:, :KV_WIDTH] + _rotate_half_partner(kn) * sin[:, :KV_WIDTH]).astype(BF16)
    vb = v.astype(BF16)
    k_all = jnp.concatenate([kprev_ref[...], kr], axis=0)
    v_all = jnp.concatenate([vprev_ref[...], vb], axis=0)
    kprev_ref[...] = kr[tm - WINDOW:, :]
    vprev_ref[...] = vb[tm - WINDOW:, :]

    rows = ATT_GROUP * WINDOW
    r = lax.broadcasted_iota(jnp.int32, (rows, 2 * WINDOW), 0) & (WINDOW - 1)
    c = lax.broadcasted_iota(jnp.int32, (rows, 2 * WINDOW), 1)
    band = (c > r) & (c <= r + WINDOW)
    first_lo = jnp.where(j == 0, WINDOW, 0)
    band_first = band & (c >= first_lo)
    for qi in range(tm // WINDOW):
        mask = band_first if qi == 0 else band
        r0 = qi * WINDOW
        for kh in range(ATT_KV_HEADS):
            heads = [kh * ATT_GROUP + g for g in range(ATT_GROUP)]
            qs = jnp.concatenate(
                [qr[r0:r0 + WINDOW, hd * ATT_HEAD_DIM:(hd + 1) * ATT_HEAD_DIM] for hd in heads], axis=0)
            kw = k_all[r0:r0 + 2 * WINDOW, kh * ATT_HEAD_DIM:(kh + 1) * ATT_HEAD_DIM]
            vw = v_all[r0:r0 + 2 * WINDOW, kh * ATT_HEAD_DIM:(kh + 1) * ATT_HEAD_DIM]
            s = jnp.where(mask, _dot_nt(qs, kw), MASKED_SCORE)
            sink = jnp.concatenate(
                [jnp.full((WINDOW, 1), sink_ref[hd], F32) for hd in heads], axis=0)
            m = jnp.maximum(jnp.max(s, axis=-1, keepdims=True), sink)
            p = jnp.exp(s - m)
            denom = jnp.sum(p, axis=-1, keepdims=True) + jnp.exp(sink - m)
            o = _dot(p.astype(BF16), vw) / denom
            for g, hd in enumerate(heads):
                cat_ref[r0:r0 + WINDOW, hd * ATT_HEAD_DIM:(hd + 1) * ATT_HEAD_DIM] = (
                    o[g * WINDOW:(g + 1) * WINDOW, :].astype(BF16))

    u_ext = jnp.concatenate([uhist_ref[...], u], axis=0)
    uhist_ref[...] = u[tm - POOL_HISTORY:, :]
    t_pos = j * tm + lax.broadcasted_iota(jnp.int32, (tm, POOL_GROUP), 0)
    pooled = []
    for gi, w in enumerate(POOL_WINDOWS):
        z = u_ext[:, gi * POOL_GROUP:(gi + 1) * POOL_GROUP]
        shift = 1
        while shift < w:
            z = z + pltpu.roll(z, shift, axis=0)
            shift *= 2
        count = jnp.minimum(t_pos + 1, w).astype(F32)
        mean = z[POOL_HISTORY:, :] / count
        pooled.append((mean - u[:, gi * POOL_GROUP:(gi + 1) * POOL_GROUP]).astype(BF16))
    pm = jnp.concatenate(pooled, axis=1)
    cat_ref[:, ATT_WIDTH:] = (_dot(pm, wpool_ref[...]) * pscale_ref[...]).astype(BF16)

    o_ref[...] = x + _dot(cat_ref[...], w_out_ref[...])


def _attn_pool_mixer(x3, gain, w_in, w_out, q_gain, k_gain, sinks, pool_w, pool_scale, cos_t, sin_t):
    b, t, d = x3.shape
    tm = min(AB_ROW_TILE, t)
    ab_in = w_in.shape[1]
    head_of_lane = np.arange(ATT_WIDTH) // ATT_HEAD_DIM
    bd = jnp.asarray((head_of_lane[:, None] == head_of_lane[None, :]) / ATT_HEAD_DIM, BF16)
    wpool_bd = jax.scipy.linalg.block_diag(*[pool_w[g] for g in range(len(POOL_WINDOWS))]).astype(BF16)
    qg = jnp.tile(q_gain, ATT_HEADS).reshape(1, ATT_WIDTH)
    kg = jnp.tile(k_gain, ATT_KV_HEADS).reshape(1, KV_WIDTH)
    return pl.pallas_call(
        _ab_kernel,
        out_shape=jax.ShapeDtypeStruct((b, t, d), F32),
        grid=(b, t // tm),
        in_specs=[
            pl.BlockSpec(memory_space=pltpu.SMEM),
            pl.BlockSpec((None, tm, d), lambda i, j: (i, j, 0)),
            _const_spec((1, d)),
            _const_spec((d, ab_in)),
            pl.BlockSpec((tm, ATT_WIDTH), lambda i, j: (j, 0)),
            pl.BlockSpec((tm, ATT_WIDTH), lambda i, j: (j, 0)),
            _const_spec((1, ATT_WIDTH)),
            _const_spec((1, KV_WIDTH)),
            _const_spec((ATT_WIDTH, ATT_WIDTH)),
            _const_spec((POOL_WIDTH, POOL_WIDTH)),
            _const_spec((1, POOL_WIDTH)),
            _const_spec((ATT_WIDTH + POOL_WIDTH, d)),
        ],
        out_specs=pl.BlockSpec((None, tm, d), lambda i, j: (i, j, 0)),
        scratch_shapes=[
            pltpu.VMEM((WINDOW, KV_WIDTH), BF16),
            pltpu.VMEM((WINDOW, KV_WIDTH), BF16),
            pltpu.VMEM((POOL_HISTORY, POOL_WIDTH), F32),
            pltpu.VMEM((tm, ATT_WIDTH + POOL_WIDTH), BF16),
        ],
        compiler_params=pltpu.CompilerParams(
            dimension_semantics=("arbitrary", "arbitrary"), vmem_limit_bytes=VMEM_LIMIT_BYTES),
        name="attn_pool_mixer",
    )(sinks, x3, gain.reshape(1, d), w_in.astype(BF16), cos_t, sin_t, qg, kg, bd, wpool_bd,
      pool_scale.reshape(1, POOL_WIDTH), w_out.astype(BF16))


def _split_bf16(a):
    hi = a.astype(BF16)
    lo = (a - hi.astype(F32)).astype(BF16)
    return hi, lo


def _node_reference(g, node, row):
    n_rows, width = g.shape
    g3 = g.reshape(n_rows // node, node, width)
    return jnp.broadcast_to(g3[:, row:row + 1, :], g3.shape).reshape(n_rows, width)


def _hgrn_kernel(layer, x_ref, gain_ref, w_in_ref, lb_logits_ref, og_ref, w_out_ref, o_ref,
                 proj_ref, state_ref, y_ref):
    tm = x_ref.shape[0]
    kw = HGRN_HEADS * HGRN_DK
    cl = HGRN_CHUNK
    j = pl.program_id(1)

    @pl.when(j == 0)
    def _():
        state_ref[...] = jnp.zeros_like(state_ref)

    x = x_ref[...]
    h = _rms_norm_rows(x, gain_ref[...]).astype(BF16)
    proj_ref[...] = _dot(h, w_in_ref[...])

    logits = lb_logits_ref[...]
    e = jnp.exp(logits - jnp.max(logits, axis=0, keepdims=True))
    prob = e / jnp.sum(e, axis=0, keepdims=True)
    lb = jnp.sum(prob[:layer + 1, :], axis=0, keepdims=True) - prob[0:1, :]

    t_idx = lax.broadcasted_iota(jnp.int32, (cl, cl), 0)
    s_idx = lax.broadcasted_iota(jnp.int32, (cl, cl), 1)
    tri = (s_idx <= t_idx).astype(BF16)
    base = HGRN_BASE_BLOCK
    diag_mask = ((t_idx // base) == (s_idx // base)) & (s_idx <= t_idx)
    node_sizes = []
    n = 2 * base
    while n <= cl:
        node_sizes.append(n)
        n *= 2
    node_masks = [(t_idx // n) == (s_idx // n) for n in node_sizes]
    row_in_chunk = lax.broadcasted_iota(jnp.int32, (cl, HGRN_DK), 0)

    for c in range(tm // cl):
        r0 = c * cl
        z = proj_ref[r0:r0 + cl, kw:2 * kw]
        f = lb + (1.0 - lb) * jax.nn.sigmoid(z)
        log_f = jnp.log(jnp.maximum(f, GATE_EPS))
        lf_hi, lf_lo = _split_bf16(log_f)
        g_all = _dot(tri, lf_hi) + _dot(tri, lf_lo)
        for hd in range(HGRN_HEADS):
            cs = slice(hd * HGRN_DK, (hd + 1) * HGRN_DK)
            g = g_all[:, cs]
            qh = proj_ref[r0:r0 + cl, hd * HGRN_DK:(hd + 1) * HGRN_DK]
            qf = qh * jax.nn.sigmoid(qh)
            key = 1.0 - f[:, cs]
            vv = proj_ref[r0:r0 + cl, 2 * kw + hd * HGRN_DK:2 * kw + (hd + 1) * HGRN_DK].astype(BF16)
            gate = proj_ref[r0:r0 + cl, 3 * kw + hd * HGRN_DK:3 * kw + (hd + 1) * HGRN_DK]

            ref0 = _node_reference(g, base, base // 2 - 1)
            a = jnp.where(diag_mask,
                          _dot_nt((qf * jnp.exp(g - ref0)).astype(BF16),
                                  (key * jnp.exp(ref0 - g)).astype(BF16)), 0.0)
            for n, nmask in zip(node_sizes, node_masks):
                refn = _node_reference(g, n, n // 2 - 1)
                right = (row_in_chunk & (n // 2)) != 0
                qt = qf * jnp.exp(jnp.where(right, g - refn, MASKED_SCORE))
                kt = key * jnp.exp(jnp.where(right, MASKED_SCORE, refn - g))
                lvl = _dot_nt(qt.astype(BF16), kt.astype(BF16))
                a = a + (lvl if n == cl else jnp.where(nmask, lvl, 0.0))

            st = state_ref[hd]
            g_last = g[cl - 1:cl, :]
            o = _dot(a.astype(BF16), vv) + _dot_nt((qf * jnp.exp(g)).astype(BF16), st.astype(BF16))
            k_tail = (key * jnp.exp(g_last - g)).astype(BF16)
            state_ref[hd] = st * jnp.exp(g_last) + _dot_tn(vv, k_tail)

            og = o * jax.nn.sigmoid(gate)
            y = og * lax.rsqrt(jnp.mean(og * og, axis=-1, keepdims=True) + NORM_EPS) * og_ref[...]
            y_ref[r0:r0 + cl, cs] = y.astype(BF16)

    o_ref[...] = x + _dot(y_ref[...], w_out_ref[...])


def _hgrn_mixer(x3, gain, w_in, w_out, out_gain, lb_logits, layer):
    b, t, d = x3.shape
    tm = min(HGRN_ROW_TILE, t)
    n_in = w_in.shape[1]
    n_layers, kw = lb_logits.shape
    return pl.pallas_call(
        functools.partial(_hgrn_kernel, layer),
        out_shape=jax.ShapeDtypeStruct((b, t, d), F32),
        grid=(b, t // tm),
        in_specs=[
            pl.BlockSpec((None, tm, d), lambda i, j: (i, j, 0)),
            _const_spec((1, d)),
            _const_spec((d, n_in)),
            _const_spec((n_layers, kw)),
            _const_spec((1, HGRN_DK)),
            _const_spec((kw, d)),
        ],
        out_specs=pl.BlockSpec((None, tm, d), lambda i, j: (i, j, 0)),
        scratch_shapes=[
            pltpu.VMEM((tm, n_in), F32),
            pltpu.VMEM((HGRN_HEADS, HGRN_DK, HGRN_DK), F32),
            pltpu.VMEM((tm, kw), BF16),
        ],
        compiler_params=pltpu.CompilerParams(
            dimension_semantics=("arbitrary", "arbitrary"), vmem_limit_bytes=VMEM_LIMIT_BYTES),
        name="hgrn_mixer",
    )(x3, gain.reshape(1, d), w_in.astype(BF16), lb_logits, out_gain.reshape(1, HGRN_DK),
      w_out.astype(BF16))


def kernel(x, positions, norm_gains, ffn_w_gate, ffn_w_up, ffn_w_down, ab_w_in, ab_w_out, q_norm_gain, k_norm_gain, attn_sinks, pool_w, pool_scale, c_w_in, c_w_out, c_out_norm_gain, lb_logits):
    b, t, d = x.shape
    depth = norm_gains.shape[0]
    cos_t, sin_t = _rope_tables(positions)

    def ffn(x3, layer, which):
        return _ffn(x3.reshape(b * t, d), norm_gains[layer, 2 * which],
                    ffn_w_gate[layer, which].astype(BF16), ffn_w_up[layer, which].astype(BF16),
                    ffn_w_down[layer, which].astype(BF16)).reshape(b, t, d)

    for layer in range(depth):
        x = ffn(x, layer, 0)
        jx = layer // 2
        if layer % 2 == 0:
            x = _attn_pool_mixer(x, norm_gains[layer, 1], ab_w_in[jx], ab_w_out[jx], q_norm_gain[jx],
                                 k_norm_gain[jx], attn_sinks[jx], pool_w[jx], pool_scale[jx], cos_t, sin_t)
        else:
            x = _hgrn_mixer(x, norm_gains[layer, 1], c_w_in[jx], c_w_out[jx], c_out_norm_gain[jx],
                            lb_logits, jx)
        x = ffn(x, layer, 1)
    return x
```

```python
import functools

import jax
import jax.numpy as jnp
import numpy as np
from jax import lax
from jax.experimental import pallas as pl
from jax.experimental.pallas import tpu as pltpu

F32 = jnp.float32
BF16 = jnp.bfloat16

NORM_EPS = 1e-6
GATE_EPS = 1e-6
ROPE_THETA = 10000.0

ATT_HEADS = 8
ATT_KV_HEADS = 2
ATT_GROUP = ATT_HEADS // ATT_KV_HEADS
ATT_HEAD_DIM = 64
WINDOW = 128
ATT_WIDTH = ATT_HEADS * ATT_HEAD_DIM
KV_WIDTH = ATT_KV_HEADS * ATT_HEAD_DIM
POOL_WINDOWS = (2, 4, 8, 16)
POOL_GROUP = 128
POOL_WIDTH = POOL_GROUP * len(POOL_WINDOWS)
POOL_HISTORY = 16
HGRN_HEADS = 8
HGRN_DK = 128

FFN_ROW_TILE = 512
FFN_COL_CHUNK = 256
AB_ROW_TILE = 512
HGRN_ROW_TILE = 256
HGRN_CHUNK = 128
HGRN_BASE_BLOCK = 8
VMEM_LIMIT_BYTES = 56 * 1024 * 1024
MASKED_SCORE = -1e30


def _rms_norm_rows(x, gain):
    ms = jnp.mean(x * x, axis=-1, keepdims=True)
    return x * lax.rsqrt(ms + NORM_EPS) * gain


def _const_spec(shape):
    nd = len(shape)
    return pl.BlockSpec(shape, lambda *_: (0,) * nd, pipeline_mode=pl.Buffered(1))


def _dot(a, b):
    return jnp.dot(a, b, preferred_element_type=F32)


def _dot_nt(a, b):
    return lax.dot_general(a, b, (((1,), (1,)), ((), ())), preferred_element_type=F32)


def _dot_tn(a, b):
    return lax.dot_general(a, b, (((0,), (0,)), ((), ())), preferred_element_type=F32)


def _ffn_kernel(x_ref, gain_ref, wg_ref, wu_ref, wd_ref, o_ref):
    x = x_ref[...]
    h = _rms_norm_rows(x, gain_ref[...]).astype(BF16)
    d_ff = wg_ref.shape[1]
    acc = None
    for c in range(d_ff // FFN_COL_CHUNK):
        sl = slice(c * FFN_COL_CHUNK, (c + 1) * FFN_COL_CHUNK)
        g = _dot(h, wg_ref[:, sl])
        u = _dot(h, wu_ref[:, sl])
        a = (g * jax.nn.sigmoid(g) * u).astype(BF16)
        d = _dot(a, wd_ref[sl, :])
        acc = d if acc is None else acc + d
    o_ref[...] = x + 0.5 * acc


def _ffn(x2, gain, wg, wu, wd):
    n, d = x2.shape
    d_ff = wg.shape[1]
    tm = min(FFN_ROW_TILE, n)
    return pl.pallas_call(
        _ffn_kernel,
        out_shape=jax.ShapeDtypeStruct((n, d), F32),
        grid=(n // tm,),
        in_specs=[
            pl.BlockSpec((tm, d), lambda i: (i, 0)),
            _const_spec((1, d)),
            _const_spec((d, d_ff)),
            _const_spec((d, d_ff)),
            _const_spec((d_ff, d)),
        ],
        out_specs=pl.BlockSpec((tm, d), lambda i: (i, 0)),
        compiler_params=pltpu.CompilerParams(
            dimension_semantics=("arbitrary",), vmem_limit_bytes=VMEM_LIMIT_BYTES),
        name="ffn",
    )(x2, gain.reshape(1, d), wg, wu, wd)


def _rope_table_kernel(pos_ref, inv_freq_ref, sign_ref, cos_ref, sin_ref):
    ang = pos_ref[...].astype(F32) * inv_freq_ref[...]
    cos_ref[...] = jnp.cos(ang)
    sin_ref[...] = jnp.sin(ang) * sign_ref[...]


def _rope_tables(positions):
    t = positions.shape[0]
    half = ATT_HEAD_DIM // 2
    lane = np.arange(ATT_WIDTH)
    inv_freq = ROPE_THETA ** (-jnp.arange(half, dtype=F32) / half)
    inv_freq_lanes = jnp.tile(inv_freq, ATT_WIDTH // half).reshape(1, ATT_WIDTH)
    sign = jnp.asarray(np.where(lane % ATT_HEAD_DIM < half, -1.0, 1.0), F32).reshape(1, ATT_WIDTH)
    tr = min(256, t)
    return pl.pallas_call(
        _rope_table_kernel,
        out_shape=(jax.ShapeDtypeStruct((t, ATT_WIDTH), F32),) * 2,
        grid=(t // tr,),
        in_specs=[pl.BlockSpec((tr, 1), lambda i: (i, 0)),
                  _const_spec((1, ATT_WIDTH)), _const_spec((1, ATT_WIDTH))],
        out_specs=(pl.BlockSpec((tr, ATT_WIDTH), lambda i: (i, 0)),) * 2,
        compiler_params=pltpu.CompilerParams(dimension_semantics=("arbitrary",)),
        name="rope_tables",
    )(positions.reshape(t, 1), inv_freq_lanes, sign)


def _rotate_half_partner(z):
    w = z.shape[1]
    lane = lax.broadcasted_iota(jnp.int32, z.shape, 1)
    first_half = (lane & (ATT_HEAD_DIM // 2)) == 0
    return jnp.where(first_half, pltpu.roll(z, w - ATT_HEAD_DIM // 2, axis=1),
                     pltpu.roll(z, ATT_HEAD_DIM // 2, axis=1))


def _ab_kernel(x_ref, gain_ref, w_in_ref, cos_ref, sin_ref, qg_ref, kg_ref, bd_ref, bias_ref,
               sink_ref, wpool_ref, pscale_ref, w_out_ref, o_ref,
               kprev_ref, vt_ref, uhist_ref, at_ref):
    tm = x_ref.shape[0]
    j = pl.program_id(1)

    @pl.when(j == 0)
    def _():
        kprev_ref[...] = jnp.zeros_like(kprev_ref)
        vt_ref[:, :WINDOW] = jnp.zeros((KV_WIDTH, WINDOW), BF16)
        uhist_ref[...] = jnp.zeros_like(uhist_ref)

    @pl.when(j > 0)
    def _():
        vt_ref[:, :WINDOW] = vt_ref[:, tm:]

    x = x_ref[...]
    h = _rms_norm_rows(x, gain_ref[...]).astype(BF16)
    proj = _dot(h, w_in_ref[...])
    q = proj[:, :ATT_WIDTH]
    k = proj[:, ATT_WIDTH:ATT_WIDTH + KV_WIDTH]
    v = proj[:, ATT_WIDTH + KV_WIDTH:ATT_WIDTH + 2 * KV_WIDTH]
    u = proj[:, ATT_WIDTH + 2 * KV_WIDTH:]

    bd = bd_ref[...]
    q_ms = _dot((q * q).astype(BF16), bd)
    k_ms = _dot((k * k).astype(BF16), bd[:KV_WIDTH, :KV_WIDTH])
    cos = cos_ref[...]
    sin = sin_ref[...]
    qn = q * lax.rsqrt(q_ms + NORM_EPS) * qg_ref[...]
    kn = k * lax.rsqrt(k_ms + NORM_EPS) * kg_ref[...]
    scale = ATT_HEAD_DIM ** -0.5
    qr = ((qn * cos + _rotate_half_partner(qn) * sin) * scale).astype(BF16)
    kr = (kn * cos[:, :KV_WIDTH] + _rotate_half_partner(kn) * sin[:, :KV_WIDTH]).astype(BF16)
    k_all = jnp.concatenate([kprev_ref[...], kr], axis=0)
    kprev_ref[...] = kr[tm - WINDOW:, :]
    vt_ref[:, WINDOW:] = v.T.astype(BF16)

    first = jnp.where(j == 0, 1, 0)
    for kh in range(ATT_KV_HEADS):
        heads = [kh * ATT_GROUP + g for g in range(ATT_GROUP)]
        k_h = k_all[:, kh * ATT_HEAD_DIM:(kh + 1) * ATT_HEAD_DIM]
        sink = sink_ref[kh:kh + 1, :]
        for qi in range(tm // WINDOW):
            r0 = qi * WINDOW
            qs = jnp.concatenate(
                [qr[r0:r0 + WINDOW, hd * ATT_HEAD_DIM:(hd + 1) * ATT_HEAD_DIM] for hd in heads], axis=0)
            bias = bias_ref[first] if qi == 0 else bias_ref[0]
            s = _dot_nt(k_h[r0:r0 + 2 * WINDOW, :], qs) + bias
            m = jnp.maximum(jnp.max(s, axis=0, keepdims=True), sink)
            p = jnp.exp(s - m)
            denom = jnp.sum(p, axis=0, keepdims=True) + jnp.exp(sink - m)
            vt = vt_ref[kh * ATT_HEAD_DIM:(kh + 1) * ATT_HEAD_DIM, r0:r0 + 2 * WINDOW]
            o = _dot(vt, p.astype(BF16)) / denom
            for g, hd in enumerate(heads):
                at_ref[hd * ATT_HEAD_DIM:(hd + 1) * ATT_HEAD_DIM, r0:r0 + WINDOW] = (
                    o[:, g * WINDOW:(g + 1) * WINDOW].astype(BF16))

    u_ext = jnp.concatenate([uhist_ref[...], u], axis=0)
    uhist_ref[...] = u[tm - POOL_HISTORY:, :]
    t_pos = j * tm + lax.broadcasted_iota(jnp.int32, (tm, POOL_GROUP), 0)
    pooled = []
    for gi, w in enumerate(POOL_WINDOWS):
        z = u_ext[:, gi * POOL_GROUP:(gi + 1) * POOL_GROUP]
        shift = 1
        while shift < w:
            z = z + pltpu.roll(z, shift, axis=0)
            shift *= 2
        count = jnp.minimum(t_pos + 1, w).astype(F32)
        mean = z[POOL_HISTORY:, :] / count
        pooled.append((mean - u[:, gi * POOL_GROUP:(gi + 1) * POOL_GROUP]).astype(BF16))
    pm = jnp.concatenate(pooled, axis=1)
    pool_out = (_dot(pm, wpool_ref[...]) * pscale_ref[...]).astype(BF16)

    o_ref[...] = (x + _dot_tn(at_ref[...], w_out_ref[:ATT_WIDTH, :])
                  + _dot(pool_out, w_out_ref[ATT_WIDTH:, :]))


def _attention_bias():
    c = np.arange(2 * WINDOW)[:, None]
    r = np.arange(ATT_GROUP * WINDOW)[None, :] % WINDOW
    band = (c > r) & (c <= r + WINDOW)
    first = band & (c >= WINDOW)
    return jnp.asarray(np.where(np.stack([band, first]), 0.0, MASKED_SCORE), F32)


def _attn_pool_mixer(x3, gain, w_in, w_out, q_gain, k_gain, sinks, pool_w, pool_scale, cos_t, sin_t):
    b, t, d = x3.shape
    tm = min(AB_ROW_TILE, t)
    ab_in = w_in.shape[1]
    head_of_lane = np.arange(ATT_WIDTH) // ATT_HEAD_DIM
    bd = jnp.asarray((head_of_lane[:, None] == head_of_lane[None, :]) / ATT_HEAD_DIM, BF16)
    wpool_bd = jax.scipy.linalg.block_diag(*[pool_w[g] for g in range(len(POOL_WINDOWS))]).astype(BF16)
    qg = jnp.tile(q_gain, ATT_HEADS).reshape(1, ATT_WIDTH)
    kg = jnp.tile(k_gain, ATT_KV_HEADS).reshape(1, KV_WIDTH)
    sink_lanes = jnp.repeat(sinks, WINDOW).reshape(ATT_KV_HEADS, ATT_GROUP * WINDOW)
    return pl.pallas_call(
        _ab_kernel,
        out_shape=jax.ShapeDtypeStruct((b, t, d), F32),
        grid=(b, t // tm),
        in_specs=[
            pl.BlockSpec((None, tm, d), lambda i, j: (i, j, 0)),
            _const_spec((1, d)),
            _const_spec((d, ab_in)),
            pl.BlockSpec((tm, ATT_WIDTH), lambda i, j: (j, 0)),
            pl.BlockSpec((tm, ATT_WIDTH), lambda i, j: (j, 0)),
            _const_spec((1, ATT_WIDTH)),
            _const_spec((1, KV_WIDTH)),
            _const_spec((ATT_WIDTH, ATT_WIDTH)),
            _const_spec((2, 2 * WINDOW, ATT_GROUP * WINDOW)),
            _const_spec((ATT_KV_HEADS, ATT_GROUP * WINDOW)),
            _const_spec((POOL_WIDTH, POOL_WIDTH)),
            _const_spec((1, POOL_WIDTH)),
            _const_spec((ATT_WIDTH + POOL_WIDTH, d)),
        ],
        out_specs=pl.BlockSpec((None, tm, d), lambda i, j: (i, j, 0)),
        scratch_shapes=[
            pltpu.VMEM((WINDOW, KV_WIDTH), BF16),
            pltpu.VMEM((KV_WIDTH, WINDOW + tm), BF16),
            pltpu.VMEM((POOL_HISTORY, POOL_WIDTH), F32),
            pltpu.VMEM((ATT_WIDTH, tm), BF16),
        ],
        compiler_params=pltpu.CompilerParams(
            dimension_semantics=("arbitrary", "arbitrary"), vmem_limit_bytes=VMEM_LIMIT_BYTES),
        name="attn_pool_mixer",
    )(x3, gain.reshape(1, d), w_in.astype(BF16), cos_t, sin_t, qg, kg, bd, _attention_bias(),
      sink_lanes, wpool_bd, pool_scale.reshape(1, POOL_WIDTH), w_out.astype(BF16))


def _split_bf16(a):
    hi = a.astype(BF16)
    lo = (a - hi.astype(F32)).astype(BF16)
    return hi, lo


def _node_reference(g, node, row):
    n_rows, width = g.shape
    g3 = g.reshape(n_rows // node, node, width)
    return jnp.broadcast_to(g3[:, row:row + 1, :], g3.shape).reshape(n_rows, width)


def _node_halves(a, node):
    n_rows, width = a.shape
    a3 = a.reshape(n_rows // node, node, width)
    return a3[:, :node // 2, :], a3[:, node // 2:, :]


def _hgrn_kernel(layer, x_ref, gain_ref, w_in_ref, lb_logits_ref, og_ref, w_out_ref, o_ref,
                 state_ref, y_ref):
    tm = x_ref.shape[0]
    dk = HGRN_DK
    cl = HGRN_CHUNK
    j = pl.program_id(1)

    @pl.when(j == 0)
    def _():
        state_ref[...] = jnp.zeros_like(state_ref)

    x = x_ref[...]
    h = _rms_norm_rows(x, gain_ref[...]).astype(BF16)

    logits = lb_logits_ref[...]
    e = jnp.exp(logits - jnp.max(logits, axis=0, keepdims=True))
    prob = e / jnp.sum(e, axis=0, keepdims=True)
    lb = jnp.sum(prob[:layer + 1, :], axis=0, keepdims=True) - prob[0:1, :]

    t_idx = lax.broadcasted_iota(jnp.int32, (cl, cl), 0)
    s_idx = lax.broadcasted_iota(jnp.int32, (cl, cl), 1)
    tri = (s_idx <= t_idx).astype(BF16)
    base = HGRN_BASE_BLOCK
    diag_mask = ((t_idx // base) == (s_idx // base)) & (s_idx <= t_idx)
    node_sizes = []
    n = 2 * base
    while n <= cl:
        node_sizes.append(n)
        n *= 2
    node_masks = [(t_idx // n) == (s_idx // n) for n in node_sizes]
    lane_mean = jnp.full((dk, dk), 1.0 / dk, BF16)

    chunks = range(tm // cl)

    def rows(a, c):
        return a[c * cl:(c + 1) * cl]

    def stage_project(hd, _):
        return _dot(h, w_in_ref[hd])

    def stage_gates(hd, proj):
        qh = proj[:, :dk]
        lb_h = lb[:, hd * dk:(hd + 1) * dk]
        f = lb_h + (1.0 - lb_h) * jax.nn.sigmoid(proj[:, dk:2 * dk])
        return dict(qf=qh * jax.nn.sigmoid(qh), key=1.0 - f,
                    lf=_split_bf16(jnp.log2(jnp.maximum(f, GATE_EPS))),
                    v=proj[:, 2 * dk:3 * dk].astype(BF16), gate=jax.nn.sigmoid(proj[:, 3 * dk:]))

    def stage_cumsum(hd, c, w):
        w["g", c] = _dot(tri, rows(w["lf"][0], c)) + _dot(tri, rows(w["lf"][1], c))

    def stage_operands(hd, c, w):
        g, qf, key = w["g", c], rows(w["qf"], c), rows(w["key"], c)
        ref0 = _node_reference(g, base, base // 2 - 1)
        ops = [((qf * jnp.exp2(g - ref0)).astype(BF16), (key * jnp.exp2(ref0 - g)).astype(BF16))]
        for n in node_sizes:
            g_l, g_r = _node_halves(g, n)
            ref = g_l[:, n // 2 - 1:n // 2, :]
            qt_r = (_node_halves(qf, n)[1] * jnp.exp2(g_r - ref)).astype(BF16)
            kt_l = (_node_halves(key, n)[0] * jnp.exp2(ref - g_l)).astype(BF16)
            zeros = jnp.zeros_like(qt_r)
            ops.append((jnp.concatenate([zeros, qt_r], axis=1).reshape(cl, dk),
                        jnp.concatenate([kt_l, zeros], axis=1).reshape(cl, dk)))
        g_last = g[cl - 1:cl, :]
        w["levels", c] = ops
        w["q_head", c] = (qf * jnp.exp2(g)).astype(BF16)
        w["k_tail", c] = (key * jnp.exp2(g_last - g)).astype(BF16)
        w["decay", c] = jnp.exp2(g_last)

    def stage_scores(hd, c, w):
        ops = w["levels", c]
        a = jnp.where(diag_mask, _dot_nt(*ops[0]), 0.0)
        for n, nmask, (qt, kt) in zip(node_sizes, node_masks, ops[1:]):
            lvl = _dot_nt(qt, kt)
            a = a + (lvl if n == cl else jnp.where(nmask, lvl, 0.0))
        w["scores", c] = a.astype(BF16)

    def stage_recurrence(hd, c, w):
        st = state_ref[hd]
        vv = rows(w["v"], c)
        w["out", c] = _dot(w["scores", c], vv) + _dot_nt(w["q_head", c], st.astype(BF16))
        state_ref[hd] = st * w["decay", c] + _dot_tn(vv, w["k_tail", c])

    def stage_gate_norm(hd, c, w):
        og = w["out", c] * rows(w["gate"], c)
        w["gated", c] = og
        w["mean_sq", c] = _dot((og * og).astype(BF16), lane_mean)

    def stage_store(hd, c, w):
        y_ref[c * cl:(c + 1) * cl, hd * dk:(hd + 1) * dk] = (
            w["gated", c] * lax.rsqrt(w["mean_sq", c] + NORM_EPS) * og_ref[...]).astype(BF16)

    heads = range(HGRN_HEADS)
    work = [stage_gates(hd, proj) for hd, proj in enumerate([stage_project(hd, None) for hd in heads])]
    for stage in (stage_cumsum, stage_operands, stage_scores, stage_recurrence, stage_gate_norm,
                  stage_store):
        for c in chunks:
            for hd in heads:
                stage(hd, c, work[hd])

    o_ref[...] = x + _dot(y_ref[...], w_out_ref[...])


def _hgrn_mixer(x3, gain, w_in, w_out, out_gain, lb_logits, layer):
    b, t, d = x3.shape
    tm = min(HGRN_ROW_TILE, t)
    n_layers, kw = lb_logits.shape
    w_heads = w_in.astype(BF16).reshape(d, 4, HGRN_HEADS, HGRN_DK).transpose(2, 0, 1, 3).reshape(
        HGRN_HEADS, d, 4 * HGRN_DK)
    return pl.pallas_call(
        functools.partial(_hgrn_kernel, layer),
        out_shape=jax.ShapeDtypeStruct((b, t, d), F32),
        grid=(b, t // tm),
        in_specs=[
            pl.BlockSpec((None, tm, d), lambda i, j: (i, j, 0)),
            _const_spec((1, d)),
            _const_spec((HGRN_HEADS, d, 4 * HGRN_DK)),
            _const_spec((n_layers, kw)),
            _const_spec((1, HGRN_DK)),
            _const_spec((kw, d)),
        ],
        out_specs=pl.BlockSpec((None, tm, d), lambda i, j: (i, j, 0)),
        scratch_shapes=[
            pltpu.VMEM((HGRN_HEADS, HGRN_DK, HGRN_DK), F32),
            pltpu.VMEM((tm, kw), BF16),
        ],
        compiler_params=pltpu.CompilerParams(
            dimension_semantics=("arbitrary", "arbitrary"), vmem_limit_bytes=VMEM_LIMIT_BYTES),
        name="hgrn_mixer",
    )(x3, gain.reshape(1, d), w_heads, lb_logits, out_gain.reshape(1, HGRN_DK), w_out.astype(BF16))


def kernel(x, positions, norm_gains, ffn_w_gate, ffn_w_up, ffn_w_down, ab_w_in, ab_w_out, q_norm_gain, k_norm_gain, attn_sinks, pool_w, pool_scale, c_w_in, c_w_out, c_out_norm_gain, lb_logits):
    b, t, d = x.shape
    depth = norm_gains.shape[0]
    cos_t, sin_t = _rope_tables(positions)

    def ffn(x3, layer, which):
        return _ffn(x3.reshape(b * t, d), norm_gains[layer, 2 * which],
                    ffn_w_gate[layer, which].astype(BF16), ffn_w_up[layer, which].astype(BF16),
                    ffn_w_down[layer, which].astype(BF16)).reshape(b, t, d)

    for layer in range(depth):
        x = ffn(x, layer, 0)
        jx = layer // 2
        if layer % 2 == 0:
            x = _attn_pool_mixer(x, norm_gains[layer, 1], ab_w_in[jx], ab_w_out[jx], q_norm_gain[jx],
                                 k_norm_gain[jx], attn_sinks[jx], pool_w[jx], pool_scale[jx], cos_t, sin_t)
        else:
            x = _hgrn_mixer(x, norm_gains[layer, 1], c_w_in[jx], c_w_out[jx], c_out_norm_gain[jx],
                            lb_logits, jx)
        x = ffn(x, layer, 1)
    return x
```

```python
import functools

import jax
import jax.numpy as jnp
import numpy as np
from jax import lax
from jax.experimental import pallas as pl
from jax.experimental.pallas import tpu as pltpu

F32 = jnp.float32
BF16 = jnp.bfloat16

NORM_EPS = 1e-6
GATE_EPS = 1e-6
ROPE_THETA = 10000.0

ATT_HEADS = 8
ATT_KV_HEADS = 2
ATT_GROUP = ATT_HEADS // ATT_KV_HEADS
ATT_HEAD_DIM = 64
WINDOW = 128
ATT_WIDTH = ATT_HEADS * ATT_HEAD_DIM
KV_WIDTH = ATT_KV_HEADS * ATT_HEAD_DIM
POOL_WINDOWS = (2, 4, 8, 16)
POOL_GROUP = 128
POOL_WIDTH = POOL_GROUP * len(POOL_WINDOWS)
POOL_HISTORY = 16
HGRN_HEADS = 8
HGRN_DK = 128

FFN_ROW_TILE = 1024
FFN_COL_CHUNK = 256
AB_ROW_TILE = 1024
HGRN_ROW_TILE = 512
HGRN_CHUNK = 128
HGRN_BASE_BLOCK = 8
VMEM_LIMIT_BYTES = 56 * 1024 * 1024
MASKED_SCORE = -1e30


def _rms_norm_rows(x, gain):
    ms = jnp.mean(x * x, axis=-1, keepdims=True)
    return x * lax.rsqrt(ms + NORM_EPS) * gain


def _sigmoid(x):
    return 0.5 * jnp.tanh(0.5 * x) + 0.5


def _const_spec(shape):
    nd = len(shape)
    return pl.BlockSpec(shape, lambda *_: (0,) * nd, pipeline_mode=pl.Buffered(1))


def _dot(a, b):
    return jnp.dot(a, b, preferred_element_type=F32)


def _dot_nt(a, b):
    return lax.dot_general(a, b, (((1,), (1,)), ((), ())), preferred_element_type=F32)


def _dot_tn(a, b):
    return lax.dot_general(a, b, (((0,), (0,)), ((), ())), preferred_element_type=F32)


def _ffn_kernel(x_ref, gain_ref, wg_ref, wu_ref, wd_ref, o_ref):
    x = x_ref[...]
    h = _rms_norm_rows(x, gain_ref[...]).astype(BF16)
    d_ff = wg_ref.shape[1]
    acc = None
    for c in range(d_ff // FFN_COL_CHUNK):
        sl = slice(c * FFN_COL_CHUNK, (c + 1) * FFN_COL_CHUNK)
        g = _dot(h, wg_ref[:, sl])
        u = _dot(h, wu_ref[:, sl])
        a = (g * jax.nn.sigmoid(g) * u).astype(BF16)
        d = _dot(a, wd_ref[sl, :])
        acc = d if acc is None else acc + d
    o_ref[...] = x + 0.5 * acc


def _ffn(x2, gain, wg, wu, wd):
    n, d = x2.shape
    d_ff = wg.shape[1]
    tm = min(FFN_ROW_TILE, n)
    return pl.pallas_call(
        _ffn_kernel,
        out_shape=jax.ShapeDtypeStruct((n, d), F32),
        grid=(n // tm,),
        in_specs=[
            pl.BlockSpec((tm, d), lambda i: (i, 0)),
            _const_spec((1, d)),
            _const_spec((d, d_ff)),
            _const_spec((d, d_ff)),
            _const_spec((d_ff, d)),
        ],
        out_specs=pl.BlockSpec((tm, d), lambda i: (i, 0)),
        compiler_params=pltpu.CompilerParams(
            dimension_semantics=("arbitrary",), vmem_limit_bytes=VMEM_LIMIT_BYTES),
        name="ffn",
    )(x2, gain.reshape(1, d), wg, wu, wd)


def _rope_table_kernel(pos_ref, inv_freq_ref, sign_ref, cos_ref, sin_ref):
    ang = pos_ref[...].astype(F32) * inv_freq_ref[...]
    cos_ref[...] = jnp.cos(ang)
    sin_ref[...] = jnp.sin(ang) * sign_ref[...]


def _rope_tables(positions):
    t = positions.shape[0]
    half = ATT_HEAD_DIM // 2
    lane = np.arange(ATT_WIDTH)
    inv_freq = ROPE_THETA ** (-jnp.arange(half, dtype=F32) / half)
    inv_freq_lanes = jnp.tile(inv_freq, ATT_WIDTH // half).reshape(1, ATT_WIDTH)
    sign = jnp.asarray(np.where(lane % ATT_HEAD_DIM < half, -1.0, 1.0), F32).reshape(1, ATT_WIDTH)
    tr = min(256, t)
    return pl.pallas_call(
        _rope_table_kernel,
        out_shape=(jax.ShapeDtypeStruct((t, ATT_WIDTH), F32),) * 2,
        grid=(t // tr,),
        in_specs=[pl.BlockSpec((tr, 1), lambda i: (i, 0)),
                  _const_spec((1, ATT_WIDTH)), _const_spec((1, ATT_WIDTH))],
        out_specs=(pl.BlockSpec((tr, ATT_WIDTH), lambda i: (i, 0)),) * 2,
        compiler_params=pltpu.CompilerParams(dimension_semantics=("arbitrary",)),
        name="rope_tables",
    )(positions.reshape(t, 1), inv_freq_lanes, sign)


def _rotate_half_partner(z):
    w = z.shape[1]
    lane = lax.broadcasted_iota(jnp.int32, z.shape, 1)
    first_half = (lane & (ATT_HEAD_DIM // 2)) == 0
    return jnp.where(first_half, pltpu.roll(z, w - ATT_HEAD_DIM // 2, axis=1),
                     pltpu.roll(z, ATT_HEAD_DIM // 2, axis=1))


def _ab_kernel(x_ref, gain_ref, w_in_ref, cos_ref, sin_ref, qg_ref, kg_ref, bd_ref, bias_ref,
               sink_ref, wpool_ref, pscale_ref, w_out_ref, o_ref,
               kprev_ref, vtprev_ref, uhist_ref):
    tm = x_ref.shape[0]
    j = pl.program_id(1)

    @pl.when(j == 0)
    def _():
        kprev_ref[...] = jnp.zeros_like(kprev_ref)
        vtprev_ref[...] = jnp.zeros_like(vtprev_ref)
        uhist_ref[...] = jnp.zeros_like(uhist_ref)

    k_prev = kprev_ref[...]
    vt_prev = vtprev_ref[...]
    u_hist = uhist_ref[...]
    bias_rest = bias_ref[0]
    bias_first = bias_ref[jnp.where(j == 0, 1, 0)]

    x = x_ref[...]
    h = _rms_norm_rows(x, gain_ref[...]).astype(BF16)
    proj = _dot(h, w_in_ref[...])
    q = proj[:, :ATT_WIDTH]
    k = proj[:, ATT_WIDTH:ATT_WIDTH + KV_WIDTH]
    v = proj[:, ATT_WIDTH + KV_WIDTH:ATT_WIDTH + 2 * KV_WIDTH]
    u = proj[:, ATT_WIDTH + 2 * KV_WIDTH:]

    bd = bd_ref[...]
    q_ms = _dot((q * q).astype(BF16), bd)
    k_ms = _dot((k * k).astype(BF16), bd[:KV_WIDTH, :KV_WIDTH])
    cos = cos_ref[...]
    sin = sin_ref[...]
    qn = q * lax.rsqrt(q_ms + NORM_EPS) * qg_ref[...]
    kn = k * lax.rsqrt(k_ms + NORM_EPS) * kg_ref[...]
    scale = ATT_HEAD_DIM ** -0.5
    qr = ((qn * cos + _rotate_half_partner(qn) * sin) * scale).astype(BF16)
    kr = (kn * cos[:, :KV_WIDTH] + _rotate_half_partner(kn) * sin[:, :KV_WIDTH]).astype(BF16)
    k_all = jnp.concatenate([k_prev, kr], axis=0)
    vt_cur = v.T.astype(BF16)
    vt_all = jnp.concatenate([vt_prev, vt_cur], axis=1)

    blocks = [(kh, qi) for kh in range(ATT_KV_HEADS) for qi in range(tm // WINDOW)]
    sinks = [sink_ref[kh:kh + 1, :] for kh in range(ATT_KV_HEADS)]
    scores = {}
    for kh, qi in blocks:
        r0 = qi * WINDOW
        qs = jnp.concatenate(
            [qr[r0:r0 + WINDOW, (kh * ATT_GROUP + g) * ATT_HEAD_DIM:(kh * ATT_GROUP + g + 1) * ATT_HEAD_DIM]
             for g in range(ATT_GROUP)], axis=0)
        k_win = k_all[r0:r0 + 2 * WINDOW, kh * ATT_HEAD_DIM:(kh + 1) * ATT_HEAD_DIM]
        scores[kh, qi] = _dot_nt(k_win, qs) + (bias_first if qi == 0 else bias_rest)
    probs, denoms = {}, {}
    for kh, qi in blocks:
        s = scores[kh, qi]
        m = jnp.maximum(jnp.max(s, axis=0, keepdims=True), sinks[kh])
        p = jnp.exp(s - m)
        denoms[kh, qi] = jnp.sum(p, axis=0, keepdims=True) + jnp.exp(sinks[kh] - m)
        probs[kh, qi] = p.astype(BF16)
    outs = {}
    for kh, qi in blocks:
        r0 = qi * WINDOW
        vt_win = vt_all[kh * ATT_HEAD_DIM:(kh + 1) * ATT_HEAD_DIM, r0:r0 + 2 * WINDOW]
        outs[kh, qi] = (_dot(vt_win, probs[kh, qi]) / denoms[kh, qi]).astype(BF16)
    a_t = jnp.concatenate(
        [jnp.concatenate([outs[hd // ATT_GROUP, qi][:, (hd % ATT_GROUP) * WINDOW:(hd % ATT_GROUP + 1) * WINDOW]
                          for qi in range(tm // WINDOW)], axis=1)
         for hd in range(ATT_HEADS)], axis=0)

    u_ext = jnp.concatenate([u_hist, u], axis=0)
    t_pos = j * tm + lax.broadcasted_iota(jnp.int32, (tm, POOL_GROUP), 0)
    pooled = []
    for gi, w in enumerate(POOL_WINDOWS):
        z = u_ext[:, gi * POOL_GROUP:(gi + 1) * POOL_GROUP]
        shift = 1
        while shift < w:
            z = z + pltpu.roll(z, shift, axis=0)
            shift *= 2
        count = jnp.minimum(t_pos + 1, w).astype(F32)
        mean = z[POOL_HISTORY:, :] / count
        pooled.append((mean - u[:, gi * POOL_GROUP:(gi + 1) * POOL_GROUP]).astype(BF16))
    pm = jnp.concatenate(pooled, axis=1)
    pool_out = (_dot(pm, wpool_ref[...]) * pscale_ref[...]).astype(BF16)

    o_ref[...] = (x + _dot_tn(a_t, w_out_ref[:ATT_WIDTH, :])
                  + _dot(pool_out, w_out_ref[ATT_WIDTH:, :]))
    kprev_ref[...] = kr[tm - WINDOW:, :]
    vtprev_ref[...] = vt_cur[:, tm - WINDOW:]
    uhist_ref[...] = u[tm - POOL_HISTORY:, :]


def _attention_bias():
    c = np.arange(2 * WINDOW)[:, None]
    r = np.arange(ATT_GROUP * WINDOW)[None, :] % WINDOW
    band = (c > r) & (c <= r + WINDOW)
    first = band & (c >= WINDOW)
    return jnp.asarray(np.where(np.stack([band, first]), 0.0, MASKED_SCORE), F32)


def _attn_pool_mixer(x3, gain, w_in, w_out, q_gain, k_gain, sinks, pool_w, pool_scale, cos_t, sin_t):
    b, t, d = x3.shape
    tm = min(AB_ROW_TILE, t)
    ab_in = w_in.shape[1]
    head_of_lane = np.arange(ATT_WIDTH) // ATT_HEAD_DIM
    bd = jnp.asarray((head_of_lane[:, None] == head_of_lane[None, :]) / ATT_HEAD_DIM, BF16)
    wpool_bd = jax.scipy.linalg.block_diag(*[pool_w[g] for g in range(len(POOL_WINDOWS))]).astype(BF16)
    qg = jnp.tile(q_gain, ATT_HEADS).reshape(1, ATT_WIDTH)
    kg = jnp.tile(k_gain, ATT_KV_HEADS).reshape(1, KV_WIDTH)
    sink_lanes = jnp.repeat(sinks, WINDOW).reshape(ATT_KV_HEADS, ATT_GROUP * WINDOW)
    return pl.pallas_call(
        _ab_kernel,
        out_shape=jax.ShapeDtypeStruct((b, t, d), F32),
        grid=(b, t // tm),
        in_specs=[
            pl.BlockSpec((None, tm, d), lambda i, j: (i, j, 0)),
            _const_spec((1, d)),
            _const_spec((d, ab_in)),
            pl.BlockSpec((tm, ATT_WIDTH), lambda i, j: (j, 0)),
            pl.BlockSpec((tm, ATT_WIDTH), lambda i, j: (j, 0)),
            _const_spec((1, ATT_WIDTH)),
            _const_spec((1, KV_WIDTH)),
            _const_spec((ATT_WIDTH, ATT_WIDTH)),
            _const_spec((2, 2 * WINDOW, ATT_GROUP * WINDOW)),
            _const_spec((ATT_KV_HEADS, ATT_GROUP * WINDOW)),
            _const_spec((POOL_WIDTH, POOL_WIDTH)),
            _const_spec((1, POOL_WIDTH)),
            _const_spec((ATT_WIDTH + POOL_WIDTH, d)),
        ],
        out_specs=pl.BlockSpec((None, tm, d), lambda i, j: (i, j, 0)),
        scratch_shapes=[
            pltpu.VMEM((WINDOW, KV_WIDTH), BF16),
            pltpu.VMEM((KV_WIDTH, WINDOW), BF16),
            pltpu.VMEM((POOL_HISTORY, POOL_WIDTH), F32),
        ],
        compiler_params=pltpu.CompilerParams(
            dimension_semantics=("arbitrary", "arbitrary"), vmem_limit_bytes=VMEM_LIMIT_BYTES),
        name="attn_pool_mixer",
    )(x3, gain.reshape(1, d), w_in.astype(BF16), cos_t, sin_t, qg, kg, bd, _attention_bias(),
      sink_lanes, wpool_bd, pool_scale.reshape(1, POOL_WIDTH), w_out.astype(BF16))


def _split_bf16(a):
    hi = a.astype(BF16)
    lo = (a - hi.astype(F32)).astype(BF16)
    return hi, lo


def _node_reference(g, node, row):
    n_rows, width = g.shape
    g3 = g.reshape(n_rows // node, node, width)
    return jnp.broadcast_to(g3[:, row:row + 1, :], g3.shape).reshape(n_rows, width)


def _node_halves(a, node):
    n_rows, width = a.shape
    a3 = a.reshape(n_rows // node, node, width)
    return a3[:, :node // 2, :], a3[:, node // 2:, :]


def _hgrn_kernel(layer, x_ref, gain_ref, w_in_ref, lb_logits_ref, og_ref, w_out_ref, o_ref,
                 state_ref, y_ref):
    tm = x_ref.shape[0]
    dk = HGRN_DK
    cl = HGRN_CHUNK
    j = pl.program_id(1)

    @pl.when(j == 0)
    def _():
        state_ref[...] = jnp.zeros_like(state_ref)

    x = x_ref[...]
    h = _rms_norm_rows(x, gain_ref[...]).astype(BF16)

    logits = lb_logits_ref[...]
    e = jnp.exp(logits - jnp.max(logits, axis=0, keepdims=True))
    prob = e / jnp.sum(e, axis=0, keepdims=True)
    lb = jnp.sum(prob[:layer + 1, :], axis=0, keepdims=True) - prob[0:1, :]

    t_idx = lax.broadcasted_iota(jnp.int32, (cl, cl), 0)
    s_idx = lax.broadcasted_iota(jnp.int32, (cl, cl), 1)
    tri = (s_idx <= t_idx).astype(BF16)
    base = HGRN_BASE_BLOCK
    diag_mask = ((t_idx // base) == (s_idx // base)) & (s_idx <= t_idx)
    node_sizes = []
    n = 2 * base
    while n <= cl:
        node_sizes.append(n)
        n *= 2
    node_masks = [(t_idx // n) == (s_idx // n) for n in node_sizes]
    lane_mean = jnp.full((dk, dk), 1.0 / dk, BF16)

    chunks = range(tm // cl)

    def rows(a, c):
        return a[c * cl:(c + 1) * cl]

    def stage_project(hd, _):
        return _dot(h, w_in_ref[hd])

    def stage_gates(hd, proj):
        qh = proj[:, :dk]
        lb_h = lb[:, hd * dk:(hd + 1) * dk]
        f = lb_h + (1.0 - lb_h) * _sigmoid(proj[:, dk:2 * dk])
        return dict(qf=qh * _sigmoid(qh), key=1.0 - f,
                    lf=_split_bf16(jnp.log2(jnp.maximum(f, GATE_EPS))),
                    v=proj[:, 2 * dk:3 * dk].astype(BF16), gate=_sigmoid(proj[:, 3 * dk:]))

    def stage_cumsum(hd, c, w):
        w["g", c] = _dot(tri, rows(w["lf"][0], c)) + _dot(tri, rows(w["lf"][1], c))

    def stage_operands(hd, c, w):
        g, qf, key = w["g", c], rows(w["qf"], c), rows(w["key"], c)
        ref0 = _node_reference(g, base, base // 2 - 1)
        ops = [((qf * jnp.exp2(g - ref0)).astype(BF16), (key * jnp.exp2(ref0 - g)).astype(BF16))]
        for n in node_sizes:
            g_l, g_r = _node_halves(g, n)
            ref = g_l[:, n // 2 - 1:n // 2, :]
            qt_r = (_node_halves(qf, n)[1] * jnp.exp2(g_r - ref)).astype(BF16)
            kt_l = (_node_halves(key, n)[0] * jnp.exp2(ref - g_l)).astype(BF16)
            zeros = jnp.zeros_like(qt_r)
            ops.append((jnp.concatenate([zeros, qt_r], axis=1).reshape(cl, dk),
                        jnp.concatenate([kt_l, zeros], axis=1).reshape(cl, dk)))
        g_last = g[cl - 1:cl, :]
        w["levels", c] = ops
        w["q_head", c] = (qf * jnp.exp2(g)).astype(BF16)
        w["k_tail", c] = (key * jnp.exp2(g_last - g)).astype(BF16)
        w["decay", c] = jnp.exp2(g_last)

    def stage_scores(hd, c, w):
        ops = w["levels", c]
        a = jnp.where(diag_mask, _dot_nt(*ops[0]), 0.0)
        for n, nmask, (qt, kt) in zip(node_sizes, node_masks, ops[1:]):
            lvl = _dot_nt(qt, kt)
            a = a + (lvl if n == cl else jnp.where(nmask, lvl, 0.0))
        w["scores", c] = a.astype(BF16)

    def stage_recurrence(hd, c, w):
        st = states[hd]
        vv = rows(w["v"], c)
        w["out", c] = _dot(w["scores", c], vv) + _dot_nt(w["q_head", c], st.astype(BF16))
        states[hd] = st * w["decay", c] + _dot_tn(vv, w["k_tail", c])

    def stage_gate_norm(hd, c, w):
        og = w["out", c] * rows(w["gate"], c)
        w["gated", c] = og
        w["mean_sq", c] = _dot((og * og).astype(BF16), lane_mean)

    def stage_normalize(hd, c, w):
        w["y", c] = (w["gated", c] * lax.rsqrt(w["mean_sq", c] + NORM_EPS) * og_ref[...]).astype(BF16)

    heads = range(HGRN_HEADS)
    states = [state_ref[hd] for hd in heads]
    work = [stage_gates(hd, proj) for hd, proj in enumerate([stage_project(hd, None) for hd in heads])]
    for stage in (stage_cumsum, stage_operands, stage_scores, stage_recurrence, stage_gate_norm,
                  stage_normalize):
        for c in chunks:
            for hd in heads:
                stage(hd, c, work[hd])

    for hd in heads:
        state_ref[hd] = states[hd]
        for c in chunks:
            y_ref[c * cl:(c + 1) * cl, hd * dk:(hd + 1) * dk] = work[hd]["y", c]
    o_ref[...] = x + _dot(y_ref[...], w_out_ref[...])


def _hgrn_mixer(x3, gain, w_in, w_out, out_gain, lb_logits, layer):
    b, t, d = x3.shape
    tm = min(HGRN_ROW_TILE, t)
    n_layers, kw = lb_logits.shape
    w_heads = w_in.astype(BF16).reshape(d, 4, HGRN_HEADS, HGRN_DK).transpose(2, 0, 1, 3).reshape(
        HGRN_HEADS, d, 4 * HGRN_DK)
    return pl.pallas_call(
        functools.partial(_hgrn_kernel, layer),
        out_shape=jax.ShapeDtypeStruct((b, t, d), F32),
        grid=(b, t // tm),
        in_specs=[
            pl.BlockSpec((None, tm, d), lambda i, j: (i, j, 0)),
            _const_spec((1, d)),
            _const_spec((HGRN_HEADS, d, 4 * HGRN_DK)),
            _const_spec((n_layers, kw)),
            _const_spec((1, HGRN_DK)),
            _const_spec((kw, d)),
        ],
        out_specs=pl.BlockSpec((None, tm, d), lambda i, j: (i, j, 0)),
        scratch_shapes=[
            pltpu.VMEM((HGRN_HEADS, HGRN_DK, HGRN_DK), F32),
            pltpu.VMEM((tm, kw), BF16),
        ],
        compiler_params=pltpu.CompilerParams(
            dimension_semantics=("arbitrary", "arbitrary"), vmem_limit_bytes=VMEM_LIMIT_BYTES),
        name="hgrn_mixer",
    )(x3, gain.reshape(1, d), w_heads, lb_logits, out_gain.reshape(1, HGRN_DK), w_out.astype(BF16))


def kernel(x, positions, norm_gains, ffn_w_gate, ffn_w_up, ffn_w_down, ab_w_in, ab_w_out, q_norm_gain, k_norm_gain, attn_sinks, pool_w, pool_scale, c_w_in, c_w_out, c_out_norm_gain, lb_logits):
    b, t, d = x.shape
    depth = norm_gains.shape[0]
    cos_t, sin_t = _rope_tables(positions)

    def ffn(x3, layer, which):
        return _ffn(x3.reshape(b * t, d), norm_gains[layer, 2 * which],
                    ffn_w_gate[layer, which].astype(BF16), ffn_w_up[layer, which].astype(BF16),
                    ffn_w_down[layer, which].astype(BF16)).reshape(b, t, d)

    for layer in range(depth):
        x = ffn(x, layer, 0)
        jx = layer // 2
        if layer % 2 == 0:
            x = _attn_pool_mixer(x, norm_gains[layer, 1], ab_w_in[jx], ab_w_out[jx], q_norm_gain[jx],
                                 k_norm_gain[jx], attn_sinks[jx], pool_w[jx], pool_scale[jx], cos_t, sin_t)
        else:
            x = _hgrn_mixer(x, norm_gains[layer, 1], c_w_in[jx], c_w_out[jx], c_out_norm_gain[jx],
                            lb_logits, jx)
        x = ffn(x, layer, 1)
    return x
```

```python
import functools

import jax
import jax.numpy as jnp
import numpy as np
from jax import lax
from jax.experimental import pallas as pl
from jax.experimental.pallas import tpu as pltpu

F32 = jnp.float32
BF16 = jnp.bfloat16

NORM_EPS = 1e-6
GATE_EPS = 1e-6
ROPE_THETA = 10000.0

ATT_HEADS = 8
ATT_KV_HEADS = 2
ATT_GROUP = ATT_HEADS // ATT_KV_HEADS
ATT_HEAD_DIM = 64
WINDOW = 128
ATT_WIDTH = ATT_HEADS * ATT_HEAD_DIM
KV_WIDTH = ATT_KV_HEADS * ATT_HEAD_DIM
POOL_WINDOWS = (2, 4, 8, 16)
POOL_GROUP = 128
POOL_WIDTH = POOL_GROUP * len(POOL_WINDOWS)
POOL_HISTORY = 16
HGRN_HEADS = 8
HGRN_DK = 128

FFN_ROW_TILE = 1024
FFN_COL_CHUNK = 256
AB_ROW_TILE = 1024
HGRN_ROW_TILE = 512
HGRN_CHUNK = 128
HGRN_BASE_BLOCK = 8
VMEM_LIMIT_BYTES = 56 * 1024 * 1024
MASKED_SCORE = -1e30


def _rms_norm_rows(x, gain):
    ms = jnp.mean(x * x, axis=-1, keepdims=True)
    return x * lax.rsqrt(ms + NORM_EPS) * gain


def _sigmoid(x):
    return 0.5 * jnp.tanh(0.5 * x) + 0.5


def _const_spec(shape):
    nd = len(shape)
    return pl.BlockSpec(shape, lambda *_: (0,) * nd, pipeline_mode=pl.Buffered(1))


def _dot(a, b):
    return jnp.dot(a, b, preferred_element_type=F32)


def _dot_nt(a, b):
    return lax.dot_general(a, b, (((1,), (1,)), ((), ())), preferred_element_type=F32)


def _dot_tn(a, b):
    return lax.dot_general(a, b, (((0,), (0,)), ((), ())), preferred_element_type=F32)


def _ffn_kernel(x_ref, gain_ref, wg_ref, wu_ref, wd_ref, o_ref):
    x = x_ref[...]
    h = _rms_norm_rows(x, gain_ref[...]).astype(BF16)
    d_ff = wg_ref.shape[1]
    acc = None
    for c in range(d_ff // FFN_COL_CHUNK):
        sl = slice(c * FFN_COL_CHUNK, (c + 1) * FFN_COL_CHUNK)
        g = _dot(h, wg_ref[:, sl])
        u = _dot(h, wu_ref[:, sl])
        a = (g * jax.nn.sigmoid(g) * u).astype(BF16)
        d = _dot(a, wd_ref[sl, :])
        acc = d if acc is None else acc + d
    o_ref[...] = x + 0.5 * acc


def _ffn(x2, gain, wg, wu, wd, layer, which):
    n, d = x2.shape
    d_ff = wg.shape[-1]
    tm = min(FFN_ROW_TILE, n)

    def weight_spec(rows, cols):
        return pl.BlockSpec((None, None, rows, cols), lambda i: (layer, which, 0, 0),
                            pipeline_mode=pl.Buffered(1))

    return pl.pallas_call(
        _ffn_kernel,
        out_shape=jax.ShapeDtypeStruct((n, d), F32),
        grid=(n // tm,),
        in_specs=[
            pl.BlockSpec((tm, d), lambda i: (i, 0)),
            _const_spec((1, d)),
            weight_spec(d, d_ff),
            weight_spec(d, d_ff),
            weight_spec(d_ff, d),
        ],
        out_specs=pl.BlockSpec((tm, d), lambda i: (i, 0)),
        compiler_params=pltpu.CompilerParams(
            dimension_semantics=("arbitrary",), vmem_limit_bytes=VMEM_LIMIT_BYTES),
        name="ffn",
    )(x2, gain.reshape(1, d), wg, wu, wd)


def _rope_table_kernel(pos_ref, inv_freq_ref, sign_ref, cos_ref, sin_ref):
    ang = pos_ref[...].astype(F32) * inv_freq_ref[...]
    cos_ref[...] = jnp.cos(ang)
    sin_ref[...] = jnp.sin(ang) * sign_ref[...]


def _rope_tables(positions):
    t = positions.shape[0]
    half = ATT_HEAD_DIM // 2
    lane = np.arange(KV_WIDTH)
    inv_freq = ROPE_THETA ** (-jnp.arange(half, dtype=F32) / half)
    inv_freq_lanes = jnp.tile(inv_freq, KV_WIDTH // half).reshape(1, KV_WIDTH)
    sign = jnp.asarray(np.where(lane % ATT_HEAD_DIM < half, -1.0, 1.0), F32).reshape(1, KV_WIDTH)
    tr = min(256, t)
    return pl.pallas_call(
        _rope_table_kernel,
        out_shape=(jax.ShapeDtypeStruct((t, KV_WIDTH), F32),) * 2,
        grid=(t // tr,),
        in_specs=[pl.BlockSpec((tr, 1), lambda i: (i, 0)),
                  _const_spec((1, KV_WIDTH)), _const_spec((1, KV_WIDTH))],
        out_specs=(pl.BlockSpec((tr, KV_WIDTH), lambda i: (i, 0)),) * 2,
        compiler_params=pltpu.CompilerParams(dimension_semantics=("arbitrary",)),
        name="rope_tables",
    )(positions.reshape(t, 1), inv_freq_lanes, sign)


def _rotate_half_partner(z):
    w = z.shape[1]
    lane = lax.broadcasted_iota(jnp.int32, z.shape, 1)
    first_half = (lane & (ATT_HEAD_DIM // 2)) == 0
    return jnp.where(first_half, pltpu.roll(z, w - ATT_HEAD_DIM // 2, axis=1),
                     pltpu.roll(z, ATT_HEAD_DIM // 2, axis=1))


def _ab_kernel(x_ref, gain_ref, w_in_ref, cos_ref, sin_ref, qg_ref, kg_ref, bd_ref, bias_ref,
               sink_ref, wpool_ref, pscale_ref, w_out_ref, o_ref,
               kprev_ref, vtprev_ref, uhist_ref):
    tm = x_ref.shape[0]
    j = pl.program_id(1)

    @pl.when(j == 0)
    def _():
        kprev_ref[...] = jnp.zeros_like(kprev_ref)
        vtprev_ref[...] = jnp.zeros_like(vtprev_ref)
        uhist_ref[...] = jnp.zeros_like(uhist_ref)

    k_prev = kprev_ref[...]
    vt_prev = vtprev_ref[...]
    u_hist = uhist_ref[...]
    bias_rest = bias_ref[0]
    bias_first = bias_ref[jnp.where(j == 0, 1, 0)]

    x = x_ref[...]
    h = _rms_norm_rows(x, gain_ref[...]).astype(BF16)
    proj = _dot(h, w_in_ref[...])
    q = proj[:, :ATT_WIDTH]
    k = proj[:, ATT_WIDTH:ATT_WIDTH + KV_WIDTH]
    v = proj[:, ATT_WIDTH + KV_WIDTH:ATT_WIDTH + 2 * KV_WIDTH]
    u = proj[:, ATT_WIDTH + 2 * KV_WIDTH:]

    bd = bd_ref[...]
    q_ms = _dot((q * q).astype(BF16), bd)
    k_ms = _dot((k * k).astype(BF16), bd[:KV_WIDTH, :KV_WIDTH])
    cos_kv = cos_ref[...]
    sin_kv = sin_ref[...]
    cos = jnp.concatenate([cos_kv] * (ATT_WIDTH // KV_WIDTH), axis=1)
    sin = jnp.concatenate([sin_kv] * (ATT_WIDTH // KV_WIDTH), axis=1)
    qn = q * lax.rsqrt(q_ms + NORM_EPS) * qg_ref[...]
    kn = k * lax.rsqrt(k_ms + NORM_EPS) * kg_ref[...]
    scale = ATT_HEAD_DIM ** -0.5
    qr = ((qn * cos + _rotate_half_partner(qn) * sin) * scale).astype(BF16)
    kr = (kn * cos_kv + _rotate_half_partner(kn) * sin_kv).astype(BF16)
    k_all = jnp.concatenate([k_prev, kr], axis=0)
    vt_cur = v.T.astype(BF16)
    vt_all = jnp.concatenate([vt_prev, vt_cur], axis=1)

    blocks = [(kh, qi) for kh in range(ATT_KV_HEADS) for qi in range(tm // WINDOW)]
    sinks = [sink_ref[kh:kh + 1, :] for kh in range(ATT_KV_HEADS)]
    scores = {}
    for kh, qi in blocks:
        r0 = qi * WINDOW
        qs = jnp.concatenate(
            [qr[r0:r0 + WINDOW, (kh * ATT_GROUP + g) * ATT_HEAD_DIM:(kh * ATT_GROUP + g + 1) * ATT_HEAD_DIM]
             for g in range(ATT_GROUP)], axis=0)
        k_win = k_all[r0:r0 + 2 * WINDOW, kh * ATT_HEAD_DIM:(kh + 1) * ATT_HEAD_DIM]
        scores[kh, qi] = _dot_nt(k_win, qs) + (bias_first if qi == 0 else bias_rest)
    probs, denoms = {}, {}
    for kh, qi in blocks:
        s = scores[kh, qi]
        m = jnp.maximum(jnp.max(s, axis=0, keepdims=True), sinks[kh])
        p = jnp.exp(s - m)
        denoms[kh, qi] = jnp.sum(p, axis=0, keepdims=True) + jnp.exp(sinks[kh] - m)
        probs[kh, qi] = p.astype(BF16)
    outs = {}
    for kh, qi in blocks:
        r0 = qi * WINDOW
        vt_win = vt_all[kh * ATT_HEAD_DIM:(kh + 1) * ATT_HEAD_DIM, r0:r0 + 2 * WINDOW]
        outs[kh, qi] = (_dot(vt_win, probs[kh, qi]) / denoms[kh, qi]).astype(BF16)
    a_t = jnp.concatenate(
        [jnp.concatenate([outs[hd // ATT_GROUP, qi][:, (hd % ATT_GROUP) * WINDOW:(hd % ATT_GROUP + 1) * WINDOW]
                          for qi in range(tm // WINDOW)], axis=1)
         for hd in range(ATT_HEADS)], axis=0)

    u_ext = jnp.concatenate([u_hist, u], axis=0)
    t_pos = j * tm + lax.broadcasted_iota(jnp.int32, (tm, POOL_GROUP), 0)
    pooled = []
    for gi, w in enumerate(POOL_WINDOWS):
        z = u_ext[:, gi * POOL_GROUP:(gi + 1) * POOL_GROUP]
        shift = 1
        while shift < w:
            z = z + pltpu.roll(z, shift, axis=0)
            shift *= 2
        count = jnp.minimum(t_pos + 1, w).astype(F32)
        mean = z[POOL_HISTORY:, :] / count
        pooled.append((mean - u[:, gi * POOL_GROUP:(gi + 1) * POOL_GROUP]).astype(BF16))
    pm = jnp.concatenate(pooled, axis=1)
    pool_out = (_dot(pm, wpool_ref[...]) * pscale_ref[...]).astype(BF16)

    o_ref[...] = (x + _dot_tn(a_t, w_out_ref[:ATT_WIDTH, :])
                  + _dot(pool_out, w_out_ref[ATT_WIDTH:, :]))
    kprev_ref[...] = kr[tm - WINDOW:, :]
    vtprev_ref[...] = vt_cur[:, tm - WINDOW:]
    uhist_ref[...] = u[tm - POOL_HISTORY:, :]


def _attention_bias():
    c = np.arange(2 * WINDOW)[:, None]
    r = np.arange(ATT_GROUP * WINDOW)[None, :] % WINDOW
    band = (c > r) & (c <= r + WINDOW)
    first = band & (c >= WINDOW)
    return jnp.asarray(np.where(np.stack([band, first]), 0.0, MASKED_SCORE), F32)


def _attn_pool_mixer(x3, gain, w_in, w_out, q_gain, k_gain, sinks, pool_w, pool_scale, cos_t, sin_t):
    b, t, d = x3.shape
    tm = min(AB_ROW_TILE, t)
    ab_in = w_in.shape[1]
    head_of_lane = np.arange(ATT_WIDTH) // ATT_HEAD_DIM
    bd = jnp.asarray((head_of_lane[:, None] == head_of_lane[None, :]) / ATT_HEAD_DIM, BF16)
    wpool_bd = jax.scipy.linalg.block_diag(*[pool_w[g] for g in range(len(POOL_WINDOWS))]).astype(BF16)
    qg = jnp.tile(q_gain, ATT_HEADS).reshape(1, ATT_WIDTH)
    kg = jnp.tile(k_gain, ATT_KV_HEADS).reshape(1, KV_WIDTH)
    sink_lanes = jnp.repeat(sinks, WINDOW).reshape(ATT_KV_HEADS, ATT_GROUP * WINDOW)
    return pl.pallas_call(
        _ab_kernel,
        out_shape=jax.ShapeDtypeStruct((b, t, d), F32),
        grid=(b, t // tm),
        in_specs=[
            pl.BlockSpec((None, tm, d), lambda i, j: (i, j, 0)),
            _const_spec((1, d)),
            _const_spec((d, ab_in)),
            pl.BlockSpec((tm, KV_WIDTH), lambda i, j: (j, 0)),
            pl.BlockSpec((tm, KV_WIDTH), lambda i, j: (j, 0)),
            _const_spec((1, ATT_WIDTH)),
            _const_spec((1, KV_WIDTH)),
            _const_spec((ATT_WIDTH, ATT_WIDTH)),
            _const_spec((2, 2 * WINDOW, ATT_GROUP * WINDOW)),
            _const_spec((ATT_KV_HEADS, ATT_GROUP * WINDOW)),
            _const_spec((POOL_WIDTH, POOL_WIDTH)),
            _const_spec((1, POOL_WIDTH)),
            _const_spec((ATT_WIDTH + POOL_WIDTH, d)),
        ],
        out_specs=pl.BlockSpec((None, tm, d), lambda i, j: (i, j, 0)),
        scratch_shapes=[
            pltpu.VMEM((WINDOW, KV_WIDTH), BF16),
            pltpu.VMEM((KV_WIDTH, WINDOW), BF16),
            pltpu.VMEM((POOL_HISTORY, POOL_WIDTH), F32),
        ],
        compiler_params=pltpu.CompilerParams(
            dimension_semantics=("arbitrary", "arbitrary"), vmem_limit_bytes=VMEM_LIMIT_BYTES),
        name="attn_pool_mixer",
    )(x3, gain.reshape(1, d), w_in.astype(BF16), cos_t, sin_t, qg, kg, bd, _attention_bias(),
      sink_lanes, wpool_bd, pool_scale.reshape(1, POOL_WIDTH), w_out.astype(BF16))


def _split_bf16(a):
    hi = a.astype(BF16)
    lo = (a - hi.astype(F32)).astype(BF16)
    return hi, lo


def _node_reference(g, node, row):
    n_rows, width = g.shape
    g3 = g.reshape(n_rows // node, node, width)
    return jnp.broadcast_to(g3[:, row:row + 1, :], g3.shape).reshape(n_rows, width)


def _node_halves(a, node):
    n_rows, width = a.shape
    a3 = a.reshape(n_rows // node, node, width)
    return a3[:, :node // 2, :], a3[:, node // 2:, :]


def _hgrn_kernel(layer, x_ref, gain_ref, w_in_ref, lb_logits_ref, og_ref, w_out_ref, o_ref,
                 state_ref):
    tm = x_ref.shape[0]
    dk = HGRN_DK
    cl = HGRN_CHUNK
    j = pl.program_id(1)

    @pl.when(j == 0)
    def _():
        state_ref[...] = jnp.zeros_like(state_ref)

    x = x_ref[...]
    h = _rms_norm_rows(x, gain_ref[...]).astype(BF16)

    logits = lb_logits_ref[...]
    e = jnp.exp(logits - jnp.max(logits, axis=0, keepdims=True))
    prob = e / jnp.sum(e, axis=0, keepdims=True)
    lb = jnp.sum(prob[:layer + 1, :], axis=0, keepdims=True) - prob[0:1, :]

    t_idx = lax.broadcasted_iota(jnp.int32, (cl, cl), 0)
    s_idx = lax.broadcasted_iota(jnp.int32, (cl, cl), 1)
    tri = (s_idx <= t_idx).astype(BF16)
    base = HGRN_BASE_BLOCK
    diag_mask = ((t_idx // base) == (s_idx // base)) & (s_idx <= t_idx)
    node_sizes = []
    n = 2 * base
    while n <= cl:
        node_sizes.append(n)
        n *= 2
    node_masks = [(t_idx // n) == (s_idx // n) for n in node_sizes]
    lane_mean = jnp.full((dk, dk), 1.0 / dk, BF16)

    chunks = range(tm // cl)

    def rows(a, c):
        return a[c * cl:(c + 1) * cl]

    def stage_project(hd, _):
        return _dot(h, w_in_ref[hd])

    def stage_gates(hd, proj):
        qh = proj[:, :dk]
        lb_h = lb[:, hd * dk:(hd + 1) * dk]
        f = lb_h + (1.0 - lb_h) * _sigmoid(proj[:, dk:2 * dk])
        return dict(qf=qh * _sigmoid(qh), key=1.0 - f,
                    lf=_split_bf16(jnp.log2(jnp.maximum(f, GATE_EPS))),
                    v=proj[:, 2 * dk:3 * dk].astype(BF16), gate=_sigmoid(proj[:, 3 * dk:]))

    def stage_cumsum(hd, c, w):
        w["g", c] = _dot(tri, rows(w["lf"][0], c)) + _dot(tri, rows(w["lf"][1], c))

    def stage_operands(hd, c, w):
        g, qf, key = w["g", c], rows(w["qf"], c), rows(w["key"], c)
        ref0 = _node_reference(g, base, base // 2 - 1)
        ops = [((qf * jnp.exp2(g - ref0)).astype(BF16), (key * jnp.exp2(ref0 - g)).astype(BF16))]
        for n in node_sizes:
            g_l, g_r = _node_halves(g, n)
            ref = g_l[:, n // 2 - 1:n // 2, :]
            qt_r = (_node_halves(qf, n)[1] * jnp.exp2(g_r - ref)).astype(BF16)
            kt_l = (_node_halves(key, n)[0] * jnp.exp2(ref - g_l)).astype(BF16)
            zeros = jnp.zeros_like(qt_r)
            ops.append((jnp.concatenate([zeros, qt_r], axis=1).reshape(cl, dk),
                        jnp.concatenate([kt_l, zeros], axis=1).reshape(cl, dk)))
        g_last = g[cl - 1:cl, :]
        w["levels", c] = ops
        w["q_head", c] = (qf * jnp.exp2(g)).astype(BF16)
        w["k_tail", c] = (key * jnp.exp2(g_last - g)).astype(BF16)
        w["decay", c] = jnp.exp2(g_last)

    def stage_scores(hd, c, w):
        ops = w["levels", c]
        a = jnp.where(diag_mask, _dot_nt(*ops[0]), 0.0)
        for n, nmask, (qt, kt) in zip(node_sizes, node_masks, ops[1:]):
            lvl = _dot_nt(qt, kt)
            a = a + (lvl if n == cl else jnp.where(nmask, lvl, 0.0))
        w["scores", c] = a.astype(BF16)

    def stage_recurrence(hd, c, w):
        st = states[hd]
        vv = rows(w["v"], c)
        w["out", c] = _dot(w["scores", c], vv) + _dot_nt(w["q_head", c], st.astype(BF16))
        states[hd] = st * w["decay", c] + _dot_tn(vv, w["k_tail", c])

    def stage_gate_norm(hd, c, w):
        og = w["out", c] * rows(w["gate"], c)
        w["gated", c] = og
        w["mean_sq", c] = _dot((og * og).astype(BF16), lane_mean)

    def stage_normalize(hd, c, w):
        w["y", c] = (w["gated", c] * lax.rsqrt(w["mean_sq", c] + NORM_EPS) * og_ref[...]).astype(BF16)

    heads = range(HGRN_HEADS)
    states = [state_ref[hd] for hd in heads]
    work = [stage_gates(hd, proj) for hd, proj in enumerate([stage_project(hd, None) for hd in heads])]
    for stage in (stage_cumsum, stage_operands, stage_scores, stage_recurrence, stage_gate_norm,
                  stage_normalize):
        for c in chunks:
            for hd in heads:
                stage(hd, c, work[hd])

    for hd in heads:
        state_ref[hd] = states[hd]
    y = jnp.concatenate(
        [jnp.concatenate([work[hd]["y", c] for c in chunks], axis=0) for hd in heads], axis=1)
    o_ref[...] = x + _dot(y, w_out_ref[...])


def _hgrn_mixer(x3, gain, w_in, w_out, out_gain, lb_logits, layer):
    b, t, d = x3.shape
    tm = min(HGRN_ROW_TILE, t)
    n_layers, kw = lb_logits.shape
    w_heads = w_in.astype(BF16).reshape(d, 4, HGRN_HEADS, HGRN_DK).transpose(2, 0, 1, 3).reshape(
        HGRN_HEADS, d, 4 * HGRN_DK)
    return pl.pallas_call(
        functools.partial(_hgrn_kernel, layer),
        out_shape=jax.ShapeDtypeStruct((b, t, d), F32),
        grid=(b, t // tm),
        in_specs=[
            pl.BlockSpec((None, tm, d), lambda i, j: (i, j, 0)),
            _const_spec((1, d)),
            _const_spec((HGRN_HEADS, d, 4 * HGRN_DK)),
            _const_spec((n_layers, kw)),
            _const_spec((1, HGRN_DK)),
            _const_spec((kw, d)),
        ],
        out_specs=pl.BlockSpec((None, tm, d), lambda i, j: (i, j, 0)),
        scratch_shapes=[
            pltpu.VMEM((HGRN_HEADS, HGRN_DK, HGRN_DK), F32),
        ],
        compiler_params=pltpu.CompilerParams(
            dimension_semantics=("arbitrary", "arbitrary"), vmem_limit_bytes=VMEM_LIMIT_BYTES),
        name="hgrn_mixer",
    )(x3, gain.reshape(1, d), w_heads, lb_logits, out_gain.reshape(1, HGRN_DK), w_out.astype(BF16))


def kernel(x, positions, norm_gains, ffn_w_gate, ffn_w_up, ffn_w_down, ab_w_in, ab_w_out, q_norm_gain, k_norm_gain, attn_sinks, pool_w, pool_scale, c_w_in, c_w_out, c_out_norm_gain, lb_logits):
    b, t, d = x.shape
    depth = norm_gains.shape[0]
    cos_t, sin_t = _rope_tables(positions)

    wg_all, wu_all, wd_all = (w.astype(BF16) for w in (ffn_w_gate, ffn_w_up, ffn_w_down))

    def ffn(x3, layer, which):
        return _ffn(x3.reshape(b * t, d), norm_gains[layer, 2 * which], wg_all, wu_all, wd_all,
                    layer, which).reshape(b, t, d)

    for layer in range(depth):
        x = ffn(x, layer, 0)
        jx = layer // 2
        if layer % 2 == 0:
            x = _attn_pool_mixer(x, norm_gains[layer, 1], ab_w_in[jx], ab_w_out[jx], q_norm_gain[jx],
                                 k_norm_gain[jx], attn_sinks[jx], pool_w[jx], pool_scale[jx], cos_t, sin_t)
        else:
            x = _hgrn_mixer(x, norm_gains[layer, 1], c_w_in[jx], c_w_out[jx], c_out_norm_gain[jx],
                            lb_logits, jx)
        x = ffn(x, layer, 1)
    return x
```

```python
import functools

import jax
import jax.numpy as jnp
import numpy as np
from jax import lax
from jax.experimental import pallas as pl
from jax.experimental.pallas import tpu as pltpu

F32 = jnp.float32
BF16 = jnp.bfloat16

NORM_EPS = 1e-6
GATE_EPS = 1e-6
ROPE_THETA = 10000.0

ATT_HEADS = 8
ATT_KV_HEADS = 2
ATT_GROUP = ATT_HEADS // ATT_KV_HEADS
ATT_HEAD_DIM = 64
WINDOW = 128
ATT_WIDTH = ATT_HEADS * ATT_HEAD_DIM
KV_WIDTH = ATT_KV_HEADS * ATT_HEAD_DIM
POOL_WINDOWS = (2, 4, 8, 16)
POOL_GROUP = 128
POOL_WIDTH = POOL_GROUP * len(POOL_WINDOWS)
POOL_HISTORY = 16
HGRN_HEADS = 8
HGRN_DK = 128

FFN_ROW_TILE = 1024
FFN_COL_CHUNK = 256
AB_ROW_TILE = 1024
HGRN_ROW_TILE = 512
HGRN_CHUNK = 128
HGRN_BASE_BLOCK = 8
VMEM_LIMIT_BYTES = 56 * 1024 * 1024
MASKED_SCORE = -1e30


def _rms_normalize(x):
    ms = jnp.mean(x * x, axis=-1, keepdims=True)
    return x * lax.rsqrt(ms + NORM_EPS)


def _fold_rows(gain, w):
    return (gain[:, None] * w).astype(BF16)


def _sigmoid(x):
    return 0.5 * jnp.tanh(0.5 * x) + 0.5


def _const_spec(shape):
    nd = len(shape)
    return pl.BlockSpec(shape, lambda *_: (0,) * nd, pipeline_mode=pl.Buffered(1))


def _dot(a, b):
    return jnp.dot(a, b, preferred_element_type=F32)


def _dot_nt(a, b):
    return lax.dot_general(a, b, (((1,), (1,)), ((), ())), preferred_element_type=F32)


def _dot_tn(a, b):
    return lax.dot_general(a, b, (((0,), (0,)), ((), ())), preferred_element_type=F32)


def _ffn_kernel(x_ref, wg_ref, wu_ref, wd_ref, o_ref):
    x = x_ref[...]
    h = _rms_normalize(x).astype(BF16)
    d_ff = wg_ref.shape[1]
    acc = x
    for c in range(d_ff // FFN_COL_CHUNK):
        sl = slice(c * FFN_COL_CHUNK, (c + 1) * FFN_COL_CHUNK)
        g = _dot(h, wg_ref[:, sl])
        u = _dot(h, wu_ref[:, sl])
        a = (g * jax.nn.sigmoid(g) * u).astype(BF16)
        acc = acc + _dot(a, wd_ref[sl, :])
    o_ref[...] = acc


def _ffn(x2, wg, wu, wd, layer, which):
    n, d = x2.shape
    d_ff = wg.shape[-1]
    tm = min(FFN_ROW_TILE, n)

    def weight_spec(rows, cols):
        return pl.BlockSpec((None, None, rows, cols), lambda i: (layer, which, 0, 0),
                            pipeline_mode=pl.Buffered(1))

    return pl.pallas_call(
        _ffn_kernel,
        out_shape=jax.ShapeDtypeStruct((n, d), F32),
        grid=(n // tm,),
        in_specs=[
            pl.BlockSpec((tm, d), lambda i: (i, 0)),
            weight_spec(d, d_ff),
            weight_spec(d, d_ff),
            weight_spec(d_ff, d),
        ],
        out_specs=pl.BlockSpec((tm, d), lambda i: (i, 0)),
        compiler_params=pltpu.CompilerParams(
            dimension_semantics=("arbitrary",), vmem_limit_bytes=VMEM_LIMIT_BYTES),
        name="ffn",
    )(x2, wg, wu, wd)


def _rope_table_kernel(pos_ref, inv_freq_ref, sign_ref, cos_ref, sin_ref):
    ang = pos_ref[...].astype(F32) * inv_freq_ref[...]
    cos_ref[...] = jnp.cos(ang)
    sin_ref[...] = jnp.sin(ang) * sign_ref[...]


def _rope_tables(positions):
    t = positions.shape[0]
    half = ATT_HEAD_DIM // 2
    lane = np.arange(KV_WIDTH)
    inv_freq = ROPE_THETA ** (-jnp.arange(half, dtype=F32) / half)
    inv_freq_lanes = jnp.tile(inv_freq, KV_WIDTH // half).reshape(1, KV_WIDTH)
    sign = jnp.asarray(np.where(lane % ATT_HEAD_DIM < half, -1.0, 1.0), F32).reshape(1, KV_WIDTH)
    tr = min(256, t)
    return pl.pallas_call(
        _rope_table_kernel,
        out_shape=(jax.ShapeDtypeStruct((t, KV_WIDTH), F32),) * 2,
        grid=(t // tr,),
        in_specs=[pl.BlockSpec((tr, 1), lambda i: (i, 0)),
                  _const_spec((1, KV_WIDTH)), _const_spec((1, KV_WIDTH))],
        out_specs=(pl.BlockSpec((tr, KV_WIDTH), lambda i: (i, 0)),) * 2,
        compiler_params=pltpu.CompilerParams(dimension_semantics=("arbitrary",)),
        name="rope_tables",
    )(positions.reshape(t, 1), inv_freq_lanes, sign)


def _rotate_half_partner(z):
    w = z.shape[1]
    lane = lax.broadcasted_iota(jnp.int32, z.shape, 1)
    first_half = (lane & (ATT_HEAD_DIM // 2)) == 0
    return jnp.where(first_half, pltpu.roll(z, w - ATT_HEAD_DIM // 2, axis=1),
                     pltpu.roll(z, ATT_HEAD_DIM // 2, axis=1))


def _ab_kernel(x_ref, w_in_ref, cos_ref, sin_ref, qg_ref, kg_ref, bd_ref, bias_ref,
               sink_ref, wpool_ref, pscale_ref, w_out_ref, o_ref,
               kprev_ref, vtprev_ref, uhist_ref):
    tm = x_ref.shape[0]
    j = pl.program_id(1)

    @pl.when(j == 0)
    def _():
        kprev_ref[...] = jnp.zeros_like(kprev_ref)
        vtprev_ref[...] = jnp.zeros_like(vtprev_ref)
        uhist_ref[...] = jnp.zeros_like(uhist_ref)

    k_prev = kprev_ref[...]
    vt_prev = vtprev_ref[...]
    u_hist = uhist_ref[...]
    bias_rest = bias_ref[0]
    bias_first = bias_ref[jnp.where(j == 0, 1, 0)]

    x = x_ref[...]
    h = _rms_normalize(x).astype(BF16)
    proj = _dot(h, w_in_ref[...])
    q = proj[:, :ATT_WIDTH]
    k = proj[:, ATT_WIDTH:ATT_WIDTH + KV_WIDTH]
    v = proj[:, ATT_WIDTH + KV_WIDTH:ATT_WIDTH + 2 * KV_WIDTH]
    u = proj[:, ATT_WIDTH + 2 * KV_WIDTH:]

    bd = bd_ref[...]
    q_ms = _dot((q * q).astype(BF16), bd)
    k_ms = _dot((k * k).astype(BF16), bd[:KV_WIDTH, :KV_WIDTH])
    cos_kv = cos_ref[...]
    sin_kv = sin_ref[...]
    cos = jnp.concatenate([cos_kv] * (ATT_WIDTH // KV_WIDTH), axis=1)
    sin = jnp.concatenate([sin_kv] * (ATT_WIDTH // KV_WIDTH), axis=1)
    qn = q * lax.rsqrt(q_ms + NORM_EPS) * qg_ref[...]
    kn = k * lax.rsqrt(k_ms + NORM_EPS) * kg_ref[...]
    scale = ATT_HEAD_DIM ** -0.5
    qr =((qn * cos + _rotate_half_partner(qn) * sin) * scale).astype(BF16)
    kr = (kn * cos_kv + _rotate_half_partner(kn) * sin_kv).astype(BF16)
    k_all = jnp.concatenate([k_prev, kr], axis=0)
    vt_cur = v.T.astype(BF16)
    vt_all = jnp.concatenate([vt_prev, vt_cur], axis=1)

    blocks = [(kh, qi) for kh in range(ATT_KV_HEADS) for qi in range(tm // WINDOW)]
    sinks = [sink_ref[kh:kh + 1, :] for kh in range(ATT_KV_HEADS)]
    scores = {}
    for kh, qi in blocks:
        r0 = qi * WINDOW
        qs = jnp.concatenate(
            [qr[r0:r0 + WINDOW, (kh * ATT_GROUP + g) * ATT_HEAD_DIM:(kh * ATT_GROUP + g + 1) * ATT_HEAD_DIM]
             for g in range(ATT_GROUP)], axis=0)
        k_win = k_all[r0:r0 + 2 * WINDOW, kh * ATT_HEAD_DIM:(kh + 1) * ATT_HEAD_DIM]
        scores[kh, qi] = _dot_nt(k_win, qs) + (bias_first if qi == 0 else bias_rest)
    probs, denoms = {}, {}
    for kh, qi in blocks:
        s = scores[kh, qi]
        m = jnp.maximum(jnp.max(s, axis=0, keepdims=True), sinks[kh])
        p = jnp.exp(s - m)
        denoms[kh, qi] = jnp.sum(p, axis=0, keepdims=True) + jnp.exp(sinks[kh] - m)
        probs[kh, qi] = p.astype(BF16)
    outs = {}
    for kh, qi in blocks:
        r0 = qi * WINDOW
        vt_win = vt_all[kh * ATT_HEAD_DIM:(kh + 1) * ATT_HEAD_DIM, r0:r0 + 2 * WINDOW]
        outs[kh, qi] = (_dot(vt_win, probs[kh, qi]) / denoms[kh, qi]).astype(BF16)
    a_t = jnp.concatenate(
        [jnp.concatenate([outs[hd // ATT_GROUP, qi][:, (hd % ATT_GROUP) * WINDOW:(hd % ATT_GROUP + 1) * WINDOW]
                          for qi in range(tm // WINDOW)], axis=1)
         for hd in range(ATT_HEADS)], axis=0)

    u_ext = jnp.concatenate([u_hist, u], axis=0)
    t_pos = j * tm + lax.broadcasted_iota(jnp.int32, (tm, POOL_GROUP), 0)
    pooled = []
    for gi, w in enumerate(POOL_WINDOWS):
        z = u_ext[:, gi * POOL_GROUP:(gi + 1) * POOL_GROUP]
        shift = 1
        while shift < w:
            z = z + pltpu.roll(z, shift, axis=0)
            shift *= 2
        count = jnp.minimum(t_pos + 1, w).astype(F32)
        mean = z[POOL_HISTORY:, :] / count
        pooled.append((mean - u[:, gi * POOL_GROUP:(gi + 1) * POOL_GROUP]).astype(BF16))
    pm = jnp.concatenate(pooled, axis=1)
    pool_out = (_dot(pm, wpool_ref[...]) * pscale_ref[...]).astype(BF16)

    o_ref[...] = (x + _dot_tn(a_t, w_out_ref[:ATT_WIDTH, :])
                  + _dot(pool_out, w_out_ref[ATT_WIDTH:, :]))
    kprev_ref[...] = kr[tm - WINDOW:, :]
    vtprev_ref[...] = vt_cur[:, tm - WINDOW:]
    uhist_ref[...] = u[tm - POOL_HISTORY:, :]


def _attention_bias():
    c = np.arange(2 * WINDOW)[:, None]
    r = np.arange(ATT_GROUP * WINDOW)[None, :] % WINDOW
    band = (c > r) & (c <= r + WINDOW)
    first = band & (c >= WINDOW)
    return jnp.asarray(np.where(np.stack([band, first]), 0.0, MASKED_SCORE), F32)


def _attn_pool_mixer(x3, gain, w_in, w_out, q_gain, k_gain, sinks, pool_w, pool_scale, cos_t, sin_t):
    b, t, d = x3.shape
    tm = min(AB_ROW_TILE, t)
    ab_in = w_in.shape[1]
    head_of_lane = np.arange(ATT_WIDTH) // ATT_HEAD_DIM
    bd = jnp.asarray((head_of_lane[:, None] == head_of_lane[None, :]) / ATT_HEAD_DIM, BF16)
    wpool_bd = jax.scipy.linalg.block_diag(*[pool_w[g] for g in range(len(POOL_WINDOWS))]).astype(BF16)
    qg = jnp.tile(q_gain, ATT_HEADS).reshape(1, ATT_WIDTH)
    kg = jnp.tile(k_gain, ATT_KV_HEADS).reshape(1, KV_WIDTH)
    sink_lanes = jnp.repeat(sinks, WINDOW).reshape(ATT_KV_HEADS, ATT_GROUP * WINDOW)
    return pl.pallas_call(
        _ab_kernel,
        out_shape=jax.ShapeDtypeStruct((b, t, d), F32),
        grid=(b, t // tm),
        in_specs=[
            pl.BlockSpec((None, tm, d), lambda i, j: (i, j, 0)),
            _const_spec((d, ab_in)),
            pl.BlockSpec((tm, KV_WIDTH), lambda i, j: (j, 0)),
            pl.BlockSpec((tm, KV_WIDTH), lambda i, j: (j, 0)),
            _const_spec((1, ATT_WIDTH)),
            _const_spec((1, KV_WIDTH)),
            _const_spec((ATT_WIDTH, ATT_WIDTH)),
            _const_spec((2, 2 * WINDOW, ATT_GROUP * WINDOW)),
            _const_spec((ATT_KV_HEADS, ATT_GROUP * WINDOW)),
            _const_spec((POOL_WIDTH, POOL_WIDTH)),
            _const_spec((1, POOL_WIDTH)),
            _const_spec((ATT_WIDTH + POOL_WIDTH, d)),
        ],
        out_specs=pl.BlockSpec((None, tm, d), lambda i, j: (i, j, 0)),
        scratch_shapes=[
            pltpu.VMEM((WINDOW, KV_WIDTH), BF16),
            pltpu.VMEM((KV_WIDTH, WINDOW), BF16),
            pltpu.VMEM((POOL_HISTORY, POOL_WIDTH), F32),
        ],
        compiler_params=pltpu.CompilerParams(
            dimension_semantics=("arbitrary", "arbitrary"), vmem_limit_bytes=VMEM_LIMIT_BYTES),
        name="attn_pool_mixer",
    )(x3, _fold_rows(gain, w_in), cos_t, sin_t, qg, kg, bd, _attention_bias(),
      sink_lanes, wpool_bd, pool_scale.reshape(1, POOL_WIDTH), w_out.astype(BF16))


def _split_bf16(a):
    hi = a.astype(BF16)
    lo = (a - hi.astype(F32)).astype(BF16)
    return hi, lo


def _node_reference(g, node, row):
    n_rows, width = g.shape
    g3 = g.reshape(n_rows // node, node, width)
    return jnp.broadcast_to(g3[:, row:row + 1, :], g3.shape).reshape(n_rows, width)


def _node_halves(a, node):
    n_rows, width = a.shape
    a3 = a.reshape(n_rows // node, node, width)
    return a3[:, :node // 2, :], a3[:, node // 2:, :]


def _hgrn_kernel(layer, x_ref, w_in_ref, lb_logits_ref, w_out_ref, o_ref,
                 state_ref):
    tm = x_ref.shape[0]
    dk = HGRN_DK
    cl = HGRN_CHUNK
    j = pl.program_id(1)

    @pl.when(j == 0)
    def _():
        state_ref[...] = jnp.zeros_like(state_ref)

    x = x_ref[...]
    h = _rms_normalize(x).astype(BF16)

    logits = lb_logits_ref[...]
    e = jnp.exp(logits - jnp.max(logits, axis=0, keepdims=True))
    prob = e / jnp.sum(e, axis=0, keepdims=True)
    lb = jnp.sum(prob[:layer + 1, :], axis=0, keepdims=True) - prob[0:1, :]

    t_idx = lax.broadcasted_iota(jnp.int32, (cl, cl), 0)
    s_idx = lax.broadcasted_iota(jnp.int32, (cl, cl), 1)
    tri = (s_idx <= t_idx).astype(BF16)
    base = HGRN_BASE_BLOCK
    diag_mask = ((t_idx // base) == (s_idx // base)) & (s_idx <= t_idx)
    node_sizes = []
    n = 2 * base
    while n <= cl:
        node_sizes.append(n)
        n *= 2
    node_masks = [(t_idx // n) == (s_idx // n) for n in node_sizes]
    lane_mean = jnp.full((dk, dk), 1.0 / dk, BF16)

    chunks = range(tm // cl)

    def rows(a, c):
        return a[c * cl:(c + 1) * cl]

    def stage_project(hd, _):
        return _dot(h, w_in_ref[hd])

    def stage_gates(hd, proj):
        qh = proj[:, :dk]
        lb_h = lb[:, hd * dk:(hd + 1) * dk]
        f = lb_h + (1.0 - lb_h) * _sigmoid(proj[:, dk:2 * dk])
        return dict(qf=qh * _sigmoid(qh), key=1.0 - f,
                    lf=_split_bf16(jnp.log2(jnp.maximum(f, GATE_EPS))),
                    v=proj[:, 2 * dk:3 * dk].astype(BF16), gate=_sigmoid(proj[:, 3 * dk:]))

    def stage_cumsum(hd, c, w):
        w["g", c] = _dot(tri, rows(w["lf"][0], c)) + _dot(tri, rows(w["lf"][1], c))

    def stage_operands(hd, c, w):
        g, qf, key = w["g", c], rows(w["qf"], c), rows(w["key"], c)
        ref0 = _node_reference(g, base, base // 2 - 1)
        ops = [((qf * jnp.exp2(g - ref0)).astype(BF16), (key * jnp.exp2(ref0 - g)).astype(BF16))]
        for n in node_sizes:
            g_l, g_r = _node_halves(g, n)
            ref = g_l[:, n // 2 - 1:n // 2, :]
            qt_r = (_node_halves(qf, n)[1] * jnp.exp2(g_r - ref)).astype(BF16)
            kt_l = (_node_halves(key, n)[0] * jnp.exp2(ref - g_l)).astype(BF16)
            zeros = jnp.zeros_like(qt_r)
            ops.append((jnp.concatenate([zeros, qt_r], axis=1).reshape(cl, dk),
                        jnp.concatenate([kt_l, zeros], axis=1).reshape(cl, dk)))
        g_last = g[cl - 1:cl, :]
        w["levels", c] = ops
        w["q_head", c] = (qf * jnp.exp2(g)).astype(BF16)
        w["k_tail", c] = (key * jnp.exp2(g_last - g)).astype(BF16)
        w["decay", c] = jnp.exp2(g_last)

    def stage_scores(hd, c, w):
        ops = w["levels", c]
        a = jnp.where(diag_mask, _dot_nt(*ops[0]), 0.0)
        for n, nmask, (qt, kt) in zip(node_sizes, node_masks, ops[1:]):
            lvl = _dot_nt(qt, kt)
            a = a + (lvl if n == cl else jnp.where(nmask, lvl, 0.0))
        w["scores", c] = a.astype(BF16)

    def stage_recurrence(hd, c, w):
        st = states[hd]
        vv = rows(w["v"], c)
        w["out", c] = _dot(w["scores", c], vv) + _dot_nt(w["q_head", c], st.astype(BF16))
        states[hd] = st * w["decay", c] + _dot_tn(vv, w["k_tail", c])

    def stage_gate_norm(hd, c, w):
        og = w["out", c] * rows(w["gate"], c)
        w["gated", c] = og
        w["mean_sq", c] = _dot((og * og).astype(BF16), lane_mean)

    def stage_normalize(hd, c, w):
        w["y", c] = (w["gated", c] * lax.rsqrt(w["mean_sq", c] + NORM_EPS)).astype(BF16)

    heads = range(HGRN_HEADS)
    states = [state_ref[hd] for hd in heads]
    work = [stage_gates(hd, proj) for hd, proj in enumerate([stage_project(hd, None) for hd in heads])]
    for stage in (stage_cumsum, stage_operands, stage_scores, stage_recurrence, stage_gate_norm,
                  stage_normalize):
        for c in chunks:
            for hd in heads:
                stage(hd, c, work[hd])

    for hd in heads:
        state_ref[hd] = states[hd]
    y = jnp.concatenate(
        [jnp.concatenate([work[hd]["y", c] for c in chunks], axis=0) for hd in heads], axis=1)
    o_ref[...] = x + _dot(y, w_out_ref[...])


def _hgrn_mixer(x3, gain, w_in, w_out, out_gain, lb_logits, layer):
    b, t, d = x3.shape
    tm = min(HGRN_ROW_TILE, t)
    n_layers, kw = lb_logits.shape
    w_heads = _fold_rows(gain, w_in).reshape(d, 4, HGRN_HEADS, HGRN_DK).transpose(2, 0, 1, 3).reshape(
        HGRN_HEADS, d, 4 * HGRN_DK)
    return pl.pallas_call(
        functools.partial(_hgrn_kernel, layer),
        out_shape=jax.ShapeDtypeStruct((b, t, d), F32),
        grid=(b, t // tm),
        in_specs=[
            pl.BlockSpec((None, tm, d), lambda i, j: (i, j, 0)),
            _const_spec((HGRN_HEADS, d, 4 * HGRN_DK)),
            _const_spec((n_layers, kw)),
            _const_spec((kw, d)),
        ],
        out_specs=pl.BlockSpec((None, tm, d), lambda i, j: (i, j, 0)),
        scratch_shapes=[
            pltpu.VMEM((HGRN_HEADS, HGRN_DK, HGRN_DK), F32),
        ],
        compiler_params=pltpu.CompilerParams(
            dimension_semantics=("arbitrary", "arbitrary"), vmem_limit_bytes=VMEM_LIMIT_BYTES),
        name="hgrn_mixer",
    )(x3, w_heads, lb_logits, _fold_rows(jnp.tile(out_gain, HGRN_HEADS), w_out))


def kernel(x, positions, norm_gains, ffn_w_gate, ffn_w_up, ffn_w_down, ab_w_in, ab_w_out, q_norm_gain, k_norm_gain, attn_sinks, pool_w, pool_scale, c_w_in, c_w_out, c_out_norm_gain, lb_logits):
    b, t, d = x.shape
    depth = norm_gains.shape[0]
    cos_t, sin_t = _rope_tables(positions)

    ffn_gain = jnp.stack([norm_gains[:, 0], norm_gains[:, 2]], axis=1)[..., None]
    wg_all = (ffn_gain * ffn_w_gate).astype(BF16)
    wu_all = (ffn_gain * ffn_w_up).astype(BF16)
    wd_all = (0.5 * ffn_w_down).astype(BF16)

    def ffn(x3, layer, which):
        return _ffn(x3.reshape(b * t, d), wg_all, wu_all, wd_all, layer, which).reshape(b, t, d)

    for layer in range(depth):
        x = ffn(x, layer, 0)
        jx = layer // 2
        if layer % 2 == 0:
            x = _attn_pool_mixer(x, norm_gains[layer, 1], ab_w_in[jx], ab_w_out[jx], q_norm_gain[jx],
                                 k_norm_gain[jx], attn_sinks[jx], pool_w[jx], pool_scale[jx], cos_t, sin_t)
        else:
            x = _hgrn_mixer(x, norm_gains[layer, 1], c_w_in[jx], c_w_out[jx], c_out_norm_gain[jx],
                            lb_logits, jx)
        x = ffn(x, layer, 1)
    return x
```

```python
import functools

import jax
import jax.numpy as jnp
import numpy as np
from jax import lax
from jax.experimental import pallas as pl
from jax.experimental.pallas import tpu as pltpu

F32 = jnp.float32
BF16 = jnp.bfloat16

NORM_EPS = 1e-6
GATE_EPS = 1e-6
ROPE_THETA = 10000.0

ATT_HEADS = 8
ATT_KV_HEADS = 2
ATT_GROUP = ATT_HEADS // ATT_KV_HEADS
ATT_HEAD_DIM = 64
WINDOW = 128
ATT_WIDTH = ATT_HEADS * ATT_HEAD_DIM
KV_WIDTH = ATT_KV_HEADS * ATT_HEAD_DIM
POOL_WINDOWS = (2, 4, 8, 16)
POOL_GROUP = 128
POOL_WIDTH = POOL_GROUP * len(POOL_WINDOWS)
POOL_HISTORY = 16
HGRN_HEADS = 8
HGRN_DK = 128

FFN_ROW_TILE = 1024
FFN_COL_CHUNK = 256
AB_ROW_TILE = 1024
HGRN_ROW_TILE = 512
HGRN_CHUNK = 128
HGRN_BASE_BLOCK = 8
VMEM_LIMIT_BYTES = 56 * 1024 * 1024
MASKED_SCORE = -1e30


def _rms_normalize(x):
    ms = jnp.mean(x * x, axis=-1, keepdims=True)
    return x * lax.rsqrt(ms + NORM_EPS)


def _fold_rows(gain, w):
    return (gain[:, None] * w).astype(BF16)


def _sigmoid(x):
    return 0.5 * jnp.tanh(0.5 * x) + 0.5


def _const_spec(shape):
    nd = len(shape)
    return pl.BlockSpec(shape, lambda *_: (0,) * nd, pipeline_mode=pl.Buffered(1))


def _dot(a, b):
    return jnp.dot(a, b, preferred_element_type=F32)


def _dot_nt(a, b):
    return lax.dot_general(a, b, (((1,), (1,)), ((), ())), preferred_element_type=F32)


def _dot_tn(a, b):
    return lax.dot_general(a, b, (((0,), (0,)), ((), ())), preferred_element_type=F32)


def _ffn_kernel(x_ref, wg_ref, wu_ref, wd_ref, o_ref):
    x = x_ref[...]
    h = _rms_normalize(x).astype(BF16)
    d_ff = wg_ref.shape[1]
    acc = x
    for c in range(d_ff // FFN_COL_CHUNK):
        sl = slice(c * FFN_COL_CHUNK, (c + 1) * FFN_COL_CHUNK)
        g = _dot(h, wg_ref[:, sl])
        u = _dot(h, wu_ref[:, sl])
        a = (g * jax.nn.sigmoid(g) * u).astype(BF16)
        acc = acc + _dot(a, wd_ref[sl, :])
    o_ref[...] = acc


def _ffn(x2, wg, wu, wd, layer, which):
    n, d = x2.shape
    d_ff = wg.shape[-1]
    tm = min(FFN_ROW_TILE, n)

    def weight_spec(rows, cols):
        return pl.BlockSpec((None, None, rows, cols), lambda i: (layer, which, 0, 0),
                            pipeline_mode=pl.Buffered(1))

    return pl.pallas_call(
        _ffn_kernel,
        out_shape=jax.ShapeDtypeStruct((n, d), F32),
        grid=(n // tm,),
        in_specs=[
            pl.BlockSpec((tm, d), lambda i: (i, 0)),
            weight_spec(d, d_ff),
            weight_spec(d, d_ff),
            weight_spec(d_ff, d),
        ],
        out_specs=pl.BlockSpec((tm, d), lambda i: (i, 0)),
        compiler_params=pltpu.CompilerParams(
            dimension_semantics=("arbitrary",), vmem_limit_bytes=VMEM_LIMIT_BYTES),
        name="ffn",
    )(x2, wg, wu, wd)


def _rope_table_kernel(pos_ref, inv_freq_ref, sign_ref, cos_ref, sin_ref):
    ang = pos_ref[...].astype(F32) * inv_freq_ref[...]
    cos_ref[...] = jnp.cos(ang)
    sin_ref[...] = jnp.sin(ang) * sign_ref[...]


def _rope_tables(positions):
    t = positions.shape[0]
    half = ATT_HEAD_DIM // 2
    lane = np.arange(KV_WIDTH)
    inv_freq = ROPE_THETA ** (-jnp.arange(half, dtype=F32) / half)
    inv_freq_lanes = jnp.tile(inv_freq, KV_WIDTH // half).reshape(1, KV_WIDTH)
    sign = jnp.asarray(np.where(lane % ATT_HEAD_DIM < half, -1.0, 1.0), F32).reshape(1, KV_WIDTH)
    tr = min(256, t)
    return pl.pallas_call(
        _rope_table_kernel,
        out_shape=(jax.ShapeDtypeStruct((t, KV_WIDTH), F32),) * 2,
        grid=(t // tr,),
        in_specs=[pl.BlockSpec((tr, 1), lambda i: (i, 0)),
                  _const_spec((1, KV_WIDTH)), _const_spec((1, KV_WIDTH))],
        out_specs=(pl.BlockSpec((tr, KV_WIDTH), lambda i: (i, 0)),) * 2,
        compiler_params=pltpu.CompilerParams(dimension_semantics=("arbitrary",)),
        name="rope_tables",
    )(positions.reshape(t, 1), inv_freq_lanes, sign)


def _rotate_half_partner(z):
    w = z.shape[1]
    lane = lax.broadcasted_iota(jnp.int32, z.shape, 1)
    first_half = (lane & (ATT_HEAD_DIM // 2)) == 0
    return jnp.where(first_half, pltpu.roll(z, w - ATT_HEAD_DIM // 2, axis=1),
                     pltpu.roll(z, ATT_HEAD_DIM // 2, axis=1))


def _ab_kernel(x_ref, w_in_ref, cos_ref, sin_ref, qg_ref, kg_ref, bd_ref, bias_ref,
               sink_ref, wpool_ref, w_out_ref, o_ref,
               kprev_ref, vtprev_ref, uhist_ref):
    tm = x_ref.shape[0]
    j = pl.program_id(1)

    @pl.when(j == 0)
    def _():
        kprev_ref[...] = jnp.zeros_like(kprev_ref)
        vtprev_ref[...] = jnp.zeros_like(vtprev_ref)
        uhist_ref[...] = jnp.zeros_like(uhist_ref)

    k_prev = kprev_ref[...]
    vt_prev = vtprev_ref[...]
    u_hist = uhist_ref[...]
    bias_rest = bias_ref[0]
    bias_first = bias_ref[jnp.where(j == 0, 1, 0)]

    x = x_ref[...]
    h = _rms_normalize(x).astype(BF16)
    proj = _dot(h, w_in_ref[...])
    q = proj[:, :ATT_WIDTH]
    k = proj[:, ATT_WIDTH:ATT_WIDTH + KV_WIDTH]
    v = proj[:, ATT_WIDTH + KV_WIDTH:ATT_WIDTH + 2 * KV_WIDTH]
    u = proj[:, ATT_WIDTH + 2 * KV_WIDTH:]

    bd = bd_ref[...]
    q_sq = (q * q).astype(BF16)
    q_ms = jnp.concatenate(
        [_dot(q_sq[:, g * KV_WIDTH:(g + 1) * KV_WIDTH], bd) for g in range(ATT_WIDTH // KV_WIDTH)], axis=1)
    k_ms = _dot((k * k).astype(BF16), bd)
    cos_kv = cos_ref[...]
    sin_kv = sin_ref[...]
    cos = jnp.concatenate([cos_kv] * (ATT_WIDTH // KV_WIDTH), axis=1)
    sin = jnp.concatenate([sin_kv] * (ATT_WIDTH // KV_WIDTH), axis=1)
    qn = q * lax.rsqrt(q_ms + NORM_EPS) * qg_ref[...]
    kn = k * lax.rsqrt(k_ms + NORM_EPS) * kg_ref[...]
    scale = ATT_HEAD_DIM ** -0.5
    qr =((qn * cos + _rotate_half_partner(qn) * sin) * scale).astype(BF16)
    kr = (kn * cos_kv + _rotate_half_partner(kn) * sin_kv).astype(BF16)
    k_all = jnp.concatenate([k_prev, kr], axis=0)
    vt_cur = v.T.astype(BF16)
    vt_all = jnp.concatenate([vt_prev, vt_cur], axis=1)

    blocks = [(kh, qi) for kh in range(ATT_KV_HEADS) for qi in range(tm // WINDOW)]
    sinks = [sink_ref[kh:kh + 1, :] for kh in range(ATT_KV_HEADS)]
    scores = {}
    for kh, qi in blocks:
        r0 = qi * WINDOW
        qs = jnp.concatenate(
            [qr[r0:r0 + WINDOW, (kh * ATT_GROUP + g) * ATT_HEAD_DIM:(kh * ATT_GROUP + g + 1) * ATT_HEAD_DIM]
             for g in range(ATT_GROUP)], axis=0)
        k_win = k_all[r0:r0 + 2 * WINDOW, kh * ATT_HEAD_DIM:(kh + 1) * ATT_HEAD_DIM]
        scores[kh, qi] = _dot_nt(k_win, qs) + (bias_first if qi == 0 else bias_rest)
    probs, denoms = {}, {}
    for kh, qi in blocks:
        s = scores[kh, qi]
        m = jnp.maximum(jnp.max(s, axis=0, keepdims=True), sinks[kh])
        p = jnp.exp(s - m)
        denoms[kh, qi] = jnp.sum(p, axis=0, keepdims=True) + jnp.exp(sinks[kh] - m)
        probs[kh, qi] = p.astype(BF16)
    outs = {}
    for kh, qi in blocks:
        r0 = qi * WINDOW
        vt_win = vt_all[kh * ATT_HEAD_DIM:(kh + 1) * ATT_HEAD_DIM, r0:r0 + 2 * WINDOW]
        outs[kh, qi] = (_dot(vt_win, probs[kh, qi]) / denoms[kh, qi]).astype(BF16)
    a_t = jnp.concatenate(
        [jnp.concatenate([outs[hd // ATT_GROUP, qi][:, (hd % ATT_GROUP) * WINDOW:(hd % ATT_GROUP + 1) * WINDOW]
                          for qi in range(tm // WINDOW)], axis=1)
         for hd in range(ATT_HEADS)], axis=0)

    u_ext = jnp.concatenate([u_hist, u], axis=0)
    t_pos = j * tm + lax.broadcasted_iota(jnp.int32, (tm, POOL_GROUP), 0)
    pooled = []
    for gi, w in enumerate(POOL_WINDOWS):
        z = u_ext[:, gi * POOL_GROUP:(gi + 1) * POOL_GROUP]
        shift = 1
        while shift < w:
            z = z + pltpu.roll(z, shift, axis=0)
            shift *= 2
        count = jnp.minimum(t_pos + 1, w).astype(F32)
        mean = z[POOL_HISTORY:, :] / count
        pooled.append((mean - u[:, gi * POOL_GROUP:(gi + 1) * POOL_GROUP]).astype(BF16))
    pool_out = jnp.concatenate(
        [_dot(pooled[gi], wpool_ref[gi]) for gi in range(len(POOL_WINDOWS))], axis=1).astype(BF16)

    o_ref[...] = (x + _dot_tn(a_t, w_out_ref[:ATT_WIDTH, :])
                  + _dot(pool_out, w_out_ref[ATT_WIDTH:, :]))
    kprev_ref[...] = kr[tm - WINDOW:, :]
    vtprev_ref[...] = vt_cur[:, tm - WINDOW:]
    uhist_ref[...] = u[tm - POOL_HISTORY:, :]


def _attention_bias():
    c = np.arange(2 * WINDOW)[:, None]
    r = np.arange(ATT_GROUP * WINDOW)[None, :] % WINDOW
    band = (c > r) & (c <= r + WINDOW)
    first = band & (c >= WINDOW)
    return jnp.asarray(np.where(np.stack([band, first]), 0.0, MASKED_SCORE), F32)


def _attn_pool_mixer(x3, gain, w_in, w_out, q_gain, k_gain, sinks, pool_w, pool_scale, cos_t, sin_t):
    b, t, d = x3.shape
    tm = min(AB_ROW_TILE, t)
    ab_in = w_in.shape[1]
    head_of_lane = np.arange(KV_WIDTH) // ATT_HEAD_DIM
    bd = jnp.asarray((head_of_lane[:, None] == head_of_lane[None, :]) / ATT_HEAD_DIM, BF16)
    n_groups = len(POOL_WINDOWS)
    wpool = (pool_w * pool_scale.reshape(n_groups, 1, POOL_GROUP)).astype(BF16)
    qg = jnp.tile(q_gain, ATT_HEADS).reshape(1, ATT_WIDTH)
    kg = jnp.tile(k_gain, ATT_KV_HEADS).reshape(1, KV_WIDTH)
    sink_lanes = jnp.repeat(sinks, WINDOW).reshape(ATT_KV_HEADS, ATT_GROUP * WINDOW)
    return pl.pallas_call(
        _ab_kernel,
        out_shape=jax.ShapeDtypeStruct((b, t, d), F32),
        grid=(b, t // tm),
        in_specs=[
            pl.BlockSpec((None, tm, d), lambda i, j: (i, j, 0)),
            _const_spec((d, ab_in)),
            pl.BlockSpec((tm, KV_WIDTH), lambda i, j: (j, 0)),
            pl.BlockSpec((tm, KV_WIDTH), lambda i, j: (j, 0)),
            _const_spec((1, ATT_WIDTH)),
            _const_spec((1, KV_WIDTH)),
            _const_spec((KV_WIDTH, KV_WIDTH)),
            _const_spec((2, 2 * WINDOW, ATT_GROUP * WINDOW)),
            _const_spec((ATT_KV_HEADS, ATT_GROUP * WINDOW)),
            _const_spec((len(POOL_WINDOWS), POOL_GROUP, POOL_GROUP)),
            _const_spec((ATT_WIDTH + POOL_WIDTH, d)),
        ],
        out_specs=pl.BlockSpec((None, tm, d), lambda i, j: (i, j, 0)),
        scratch_shapes=[
            pltpu.VMEM((WINDOW, KV_WIDTH), BF16),
            pltpu.VMEM((KV_WIDTH, WINDOW), BF16),
            pltpu.VMEM((POOL_HISTORY, POOL_WIDTH), F32),
        ],
        compiler_params=pltpu.CompilerParams(
            dimension_semantics=("arbitrary", "arbitrary"), vmem_limit_bytes=VMEM_LIMIT_BYTES),
        name="attn_pool_mixer",
    )(x3, _fold_rows(gain, w_in), cos_t, sin_t, qg, kg, bd, _attention_bias(),
      sink_lanes, wpool, w_out.astype(BF16))


def _split_bf16(a):
    hi = a.astype(BF16)
    lo = (a - hi.astype(F32)).astype(BF16)
    return hi, lo


def _node_reference(g, node, row):
    n_rows, width = g.shape
    g3 = g.reshape(n_rows // node, node, width)
    return jnp.broadcast_to(g3[:, row:row + 1, :], g3.shape).reshape(n_rows, width)


def _node_halves(a, node):
    n_rows, width = a.shape
    a3 = a.reshape(n_rows // node, node, width)
    return a3[:, :node // 2, :], a3[:, node // 2:, :]


def _hgrn_kernel(layer, x_ref, w_in_ref, lb_logits_ref, w_out_ref, o_ref,
                 state_ref):
    tm = x_ref.shape[0]
    dk = HGRN_DK
    cl = HGRN_CHUNK
    j = pl.program_id(1)

    @pl.when(j == 0)
    def _():
        state_ref[...] = jnp.zeros_like(state_ref)

    x = x_ref[...]
    h = _rms_normalize(x).astype(BF16)

    logits = lb_logits_ref[...]
    e = jnp.exp(logits - jnp.max(logits, axis=0, keepdims=True))
    prob = e / jnp.sum(e, axis=0, keepdims=True)
    lb = jnp.sum(prob[:layer + 1, :], axis=0, keepdims=True) - prob[0:1, :]

    t_idx = lax.broadcasted_iota(jnp.int32, (cl, cl), 0)
    s_idx = lax.broadcasted_iota(jnp.int32, (cl, cl), 1)
    tri = (s_idx <= t_idx).astype(BF16)
    base = HGRN_BASE_BLOCK
    diag_mask = ((t_idx // base) == (s_idx // base)) & (s_idx <= t_idx)
    node_sizes = []
    n = 2 * base
    while n <= cl:
        node_sizes.append(n)
        n *= 2
    node_masks = [(t_idx // n) == (s_idx // n) for n in node_sizes]
    lane_mean = jnp.full((dk, dk), 1.0 / dk, BF16)

    chunks = range(tm // cl)

    def rows(a, c):
        return a[c * cl:(c + 1) * cl]

    def stage_project(hd, _):
        return _dot(h, w_in_ref[hd])

    def stage_gates(hd, proj):
        qh = proj[:, :dk]
        lb_h = lb[:, hd * dk:(hd + 1) * dk]
        f = lb_h + (1.0 - lb_h) * _sigmoid(proj[:, dk:2 * dk])
        return dict(qf=qh * _sigmoid(qh), key=1.0 - f,
                    lf=_split_bf16(jnp.log2(jnp.maximum(f, GATE_EPS))),
                    v=proj[:, 2 * dk:3 * dk].astype(BF16), gate=_sigmoid(proj[:, 3 * dk:]))

    def stage_cumsum(hd, c, w):
        both = _dot(tri, jnp.concatenate([rows(w["lf"][0], c), rows(w["lf"][1], c)], axis=1))
        w["g", c] = both[:, :dk] + both[:, dk:]

    def stage_operands(hd, c, w):
        g, qf, key = w["g", c], rows(w["qf"], c), rows(w["key"], c)
        ref0 = _node_reference(g, base, base // 2 - 1)
        ops = [((qf * jnp.exp2(g - ref0)).astype(BF16), (key * jnp.exp2(ref0 - g)).astype(BF16))]
        for n in node_sizes:
            g_l, g_r = _node_halves(g, n)
            ref = g_l[:, n // 2 - 1:n // 2, :]
            qt_r = (_node_halves(qf, n)[1] * jnp.exp2(g_r - ref)).astype(BF16)
            kt_l = (_node_halves(key, n)[0] * jnp.exp2(ref - g_l)).astype(BF16)
            zeros = jnp.zeros_like(qt_r)
            ops.append((jnp.concatenate([zeros, qt_r], axis=1).reshape(cl, dk),
                        jnp.concatenate([kt_l, zeros], axis=1).reshape(cl, dk)))
        g_last = g[cl - 1:cl, :]
        w["levels", c] = ops
        w["q_head", c] = (qf * jnp.exp2(g)).astype(BF16)
        w["k_tail", c] = (key * jnp.exp2(g_last - g)).astype(BF16)
        w["decay", c] = jnp.exp2(g_last)

    def stage_scores(hd, c, w):
        ops = w["levels", c]
        a = jnp.where(diag_mask, _dot_nt(*ops[0]), 0.0)
        for n, nmask, (qt, kt) in zip(node_sizes, node_masks, ops[1:]):
            lvl = _dot_nt(qt, kt)
            a = a + (lvl if n == cl else jnp.where(nmask, lvl, 0.0))
        w["scores", c] = a.astype(BF16)

    def stage_recurrence(hd, c, w):
        st = states[hd]
        vv = rows(w["v"], c)
        w["out", c] = _dot(w["scores", c], vv) + _dot_nt(w["q_head", c], st.astype(BF16))
        states[hd] = st * w["decay", c] + _dot_tn(vv, w["k_tail", c])

    def gate_and_normalize(c, work):
        gated = [w["out", c] * rows(w["gate"], c) for w in work]
        squares = jnp.concatenate([(og * og).astype(BF16) for og in gated], axis=0)
        mean_sq = _dot(squares, lane_mean)
        for hd, w in enumerate(work):
            w["y", c] = (gated[hd] * lax.rsqrt(rows(mean_sq, hd) + NORM_EPS)).astype(BF16)

    heads = range(HGRN_HEADS)
    states = [state_ref[hd] for hd in heads]
    work = [stage_gates(hd, proj) for hd, proj in enumerate([stage_project(hd, None) for hd in heads])]
    for stage in (stage_cumsum, stage_operands, stage_scores, stage_recurrence):
        for c in chunks:
            for hd in heads:
                stage(hd, c, work[hd])
    for c in chunks:
        gate_and_normalize(c, work)

    for hd in heads:
        state_ref[hd] = states[hd]
    y = jnp.concatenate(
        [jnp.concatenate([work[hd]["y", c] for c in chunks], axis=0) for hd in heads], axis=1)
    o_ref[...] = x + _dot(y, w_out_ref[...])


def _hgrn_mixer(x3, gain, w_in, w_out, out_gain, lb_logits, layer):
    b, t, d = x3.shape
    tm = min(HGRN_ROW_TILE, t)
    n_layers, kw = lb_logits.shape
    w_heads = _fold_rows(gain, w_in).reshape(d, 4, HGRN_HEADS, HGRN_DK).transpose(2, 0, 1, 3).reshape(
        HGRN_HEADS, d, 4 * HGRN_DK)
    return pl.pallas_call(
        functools.partial(_hgrn_kernel, layer),
        out_shape=jax.ShapeDtypeStruct((b, t, d), F32),
        grid=(b, t // tm),
        in_specs=[
            pl.BlockSpec((None, tm, d), lambda i, j: (i, j, 0)),
            _const_spec((HGRN_HEADS, d, 4 * HGRN_DK)),
            _const_spec((n_layers, kw)),
            _const_spec((kw, d)),
        ],
        out_specs=pl.BlockSpec((None, tm, d), lambda i, j: (i, j, 0)),
        scratch_shapes=[
            pltpu.VMEM((HGRN_HEADS, HGRN_DK, HGRN_DK), F32),
        ],
        compiler_params=pltpu.CompilerParams(
            dimension_semantics=("arbitrary", "arbitrary"), vmem_limit_bytes=VMEM_LIMIT_BYTES),
        name="hgrn_mixer",
    )(x3, w_heads, lb_logits, _fold_rows(jnp.tile(out_gain, HGRN_HEADS), w_out))


def kernel(x, positions, norm_gains, ffn_w_gate, ffn_w_up, ffn_w_down, ab_w_in, ab_w_out, q_norm_gain, k_norm_gain, attn_sinks, pool_w, pool_scale, c_w_in, c_w_out, c_out_norm_gain, lb_logits):
    b, t, d = x.shape
    depth = norm_gains.shape[0]
    cos_t, sin_t = _rope_tables(positions)

    ffn_gain = jnp.stack([norm_gains[:, 0], norm_gains[:, 2]], axis=1)[..., None]
    wg_all = (ffn_gain * ffn_w_gate).astype(BF16)
    wu_all = (ffn_gain * ffn_w_up).astype(BF16)
    wd_all = (0.5 * ffn_w_down).astype(BF16)

    def ffn(x3, layer, which):
        return _ffn(x3.reshape(b * t, d), wg_all, wu_all, wd_all, layer, which).reshape(b, t, d)

    for layer in range(depth):
        x = ffn(x, layer, 0)
        jx = layer // 2
        if layer % 2 == 0:
            x = _attn_pool_mixer(x, norm_gains[layer, 1], ab_w_in[jx], ab_w_out[jx], q_norm_gain[jx],
                                 k_norm_gain[jx], attn_sinks[jx], pool_w[jx], pool_scale[jx], cos_t, sin_t)
        else:
            x = _hgrn_mixer(x, norm_gains[layer, 1], c_w_in[jx], c_w_out[jx], c_out_norm_gain[jx],
                            lb_logits, jx)
        x = ffn(x, layer, 1)
    return x
```

```python
import functools

import jax
import jax.numpy as jnp
import numpy as np
from jax import lax
from jax.experimental import pallas as pl
from jax.experimental.pallas import tpu as pltpu

F32 = jnp.float32
BF16 = jnp.bfloat16

NORM_EPS = 1e-6
GATE_EPS = 1e-6
ROPE_THETA = 10000.0

ATT_HEADS = 8
ATT_KV_HEADS = 2
ATT_GROUP = ATT_HEADS // ATT_KV_HEADS
ATT_HEAD_DIM = 64
WINDOW = 128
ATT_WIDTH = ATT_HEADS * ATT_HEAD_DIM
KV_WIDTH = ATT_KV_HEADS * ATT_HEAD_DIM
POOL_WINDOWS = (2, 4, 8, 16)
POOL_GROUP = 128
POOL_WIDTH = POOL_GROUP * len(POOL_WINDOWS)
POOL_HISTORY = 16
HGRN_HEADS = 8
HGRN_DK = 128

FFN_ROW_TILE = 1024
FFN_COL_CHUNK = 256
AB_ROW_TILE = 1024
HGRN_ROW_TILE = 512
HGRN_CHUNK = 128
HGRN_BASE_BLOCK = 8
VMEM_LIMIT_BYTES = 56 * 1024 * 1024
MASKED_SCORE = -1e30


def _rms_normalize(x):
    ms = jnp.mean(x * x, axis=-1, keepdims=True)
    return x * lax.rsqrt(ms + NORM_EPS)


def _fold_rows(gain, w):
    return (gain[:, None] * w).astype(BF16)


def _sigmoid(x):
    return 0.5 * jnp.tanh(0.5 * x) + 0.5


def _const_spec(shape):
    nd = len(shape)
    return pl.BlockSpec(shape, lambda *_: (0,) * nd, pipeline_mode=pl.Buffered(1))


def _dot(a, b):
    return jnp.dot(a, b, preferred_element_type=F32)


def _dot_nt(a, b):
    return lax.dot_general(a, b, (((1,), (1,)), ((), ())), preferred_element_type=F32)


def _dot_tn(a, b):
    return lax.dot_general(a, b, (((0,), (0,)), ((), ())), preferred_element_type=F32)


def _ffn_kernel(x_ref, wg_ref, wu_ref, wd_ref, o_ref):
    x = x_ref[...]
    h = _rms_normalize(x).astype(BF16)
    d_ff = wg_ref.shape[1]
    acc = x
    for c in range(d_ff // FFN_COL_CHUNK):
        sl = slice(c * FFN_COL_CHUNK, (c + 1) * FFN_COL_CHUNK)
        g = _dot(h, wg_ref[:, sl])
        u = _dot(h, wu_ref[:, sl])
        a = (g * jax.nn.sigmoid(g) * u).astype(BF16)
        acc = acc + _dot(a, wd_ref[sl, :])
    o_ref[...] = acc


def _ffn(x2, wg, wu, wd, layer, which):
    n, d = x2.shape
    d_ff = wg.shape[-1]
    tm = min(FFN_ROW_TILE, n)

    def weight_spec(rows, cols):
        return pl.BlockSpec((None, None, rows, cols), lambda i: (layer, which, 0, 0),
                            pipeline_mode=pl.Buffered(1))

    return pl.pallas_call(
        _ffn_kernel,
        out_shape=jax.ShapeDtypeStruct((n, d), F32),
        grid=(n // tm,),
        in_specs=[
            pl.BlockSpec((tm, d), lambda i: (i, 0)),
            weight_spec(d, d_ff),
            weight_spec(d, d_ff),
            weight_spec(d_ff, d),
        ],
        out_specs=pl.BlockSpec((tm, d), lambda i: (i, 0)),
        compiler_params=pltpu.CompilerParams(
            dimension_semantics=("arbitrary",), vmem_limit_bytes=VMEM_LIMIT_BYTES),
        name="ffn",
    )(x2, wg, wu, wd)


def _rope_table_kernel(pos_ref, inv_freq_ref, sign_ref, cos_ref, sin_ref):
    ang = pos_ref[...].astype(F32) * inv_freq_ref[...]
    cos_ref[...] = jnp.cos(ang)
    sin_ref[...] = jnp.sin(ang) * sign_ref[...]


def _rope_tables(positions):
    t = positions.shape[0]
    half = ATT_HEAD_DIM // 2
    lane = np.arange(KV_WIDTH)
    inv_freq = ROPE_THETA ** (-jnp.arange(half, dtype=F32) / half)
    inv_freq_lanes = jnp.tile(inv_freq, KV_WIDTH // half).reshape(1, KV_WIDTH)
    sign = jnp.asarray(np.where(lane % ATT_HEAD_DIM < half, -1.0, 1.0), F32).reshape(1, KV_WIDTH)
    tr = min(256, t)
    return pl.pallas_call(
        _rope_table_kernel,
        out_shape=(jax.ShapeDtypeStruct((t, KV_WIDTH), F32),) * 2,
        grid=(t // tr,),
        in_specs=[pl.BlockSpec((tr, 1), lambda i: (i, 0)),
                  _const_spec((1, KV_WIDTH)), _const_spec((1, KV_WIDTH))],
        out_specs=(pl.BlockSpec((tr, KV_WIDTH), lambda i: (i, 0)),) * 2,
        compiler_params=pltpu.CompilerParams(dimension_semantics=("arbitrary",)),
        name="rope_tables",
    )(positions.reshape(t, 1), inv_freq_lanes, sign)


def _rotate_half_partner(z):
    w = z.shape[1]
    lane = lax.broadcasted_iota(jnp.int32, z.shape, 1)
    first_half = (lane & (ATT_HEAD_DIM // 2)) == 0
    return jnp.where(first_half, pltpu.roll(z, w - ATT_HEAD_DIM // 2, axis=1),
                     pltpu.roll(z, ATT_HEAD_DIM // 2, axis=1))


def _ab_kernel(x_ref, w_in_ref, cos_ref, sin_ref, qg_ref, kg_ref, bd_ref, bias_ref,
               sink_ref, wpool_ref, w_out_ref, o_ref,
               kprev_ref, vtprev_ref, uhist_ref):
    tm = x_ref.shape[0]
    j = pl.program_id(1)

    @pl.when(j == 0)
    def _():
        kprev_ref[...] = jnp.zeros_like(kprev_ref)
        vtprev_ref[...] = jnp.zeros_like(vtprev_ref)
        uhist_ref[...] = jnp.zeros_like(uhist_ref)

    k_prev = kprev_ref[...]
    vt_prev = vtprev_ref[...]
    u_hist = uhist_ref[...]
    bias_rest = bias_ref[0]
    bias_first = bias_ref[jnp.where(j == 0, 1, 0)]

    x = x_ref[...]
    h = _rms_normalize(x).astype(BF16)
    proj = _dot(h, w_in_ref[...])
    q = proj[:, :ATT_WIDTH]
    k = proj[:, ATT_WIDTH:ATT_WIDTH + KV_WIDTH]
    v = proj[:, ATT_WIDTH + KV_WIDTH:ATT_WIDTH + 2 * KV_WIDTH]
    u = proj[:, ATT_WIDTH + 2 * KV_WIDTH:]

    bd = bd_ref[...]
    q_sq = (q * q).astype(BF16)
    q_ms = jnp.concatenate(
        [_dot(q_sq[:, g * KV_WIDTH:(g + 1) * KV_WIDTH], bd) for g in range(ATT_WIDTH // KV_WIDTH)], axis=1)
    k_ms = _dot((k * k).astype(BF16), bd)
    cos_kv = cos_ref[...]
    sin_kv = sin_ref[...]
    groups = ATT_WIDTH // KV_WIDTH
    q_cos = jnp.concatenate([cos_kv * qg_ref[0:1, :]] * groups, axis=1)
    q_sin = jnp.concatenate([sin_kv * qg_ref[1:2, :]] * groups, axis=1)
    qr = ((q * q_cos + _rotate_half_partner(q) * q_sin) * lax.rsqrt(q_ms + NORM_EPS)).astype(BF16)
    kr = ((k * (cos_kv * kg_ref[0:1, :]) + _rotate_half_partner(k) * (sin_kv * kg_ref[1:2, :]))
          * lax.rsqrt(k_ms + NORM_EPS)).astype(BF16)
    k_all = jnp.concatenate([k_prev, kr], axis=0)
    vt_cur = v.T.astype(BF16)
    vt_all = jnp.concatenate([vt_prev, vt_cur], axis=1)

    blocks = [(kh, qi) for kh in range(ATT_KV_HEADS) for qi in range(tm // WINDOW)]
    sinks = [sink_ref[kh:kh + 1, :] for kh in range(ATT_KV_HEADS)]
    scores = {}
    for kh, qi in blocks:
        r0 = qi * WINDOW
        qs = jnp.concatenate(
            [qr[r0:r0 + WINDOW, (kh * ATT_GROUP + g) * ATT_HEAD_DIM:(kh * ATT_GROUP + g + 1) * ATT_HEAD_DIM]
             for g in range(ATT_GROUP)], axis=0)
        k_win = k_all[r0:r0 + 2 * WINDOW, kh * ATT_HEAD_DIM:(kh + 1) * ATT_HEAD_DIM]
        scores[kh, qi] = _dot_nt(k_win, qs) + (bias_first if qi == 0 else bias_rest)
    probs, denoms = {}, {}
    for kh, qi in blocks:
        s = scores[kh, qi]
        m = jnp.maximum(jnp.max(s, axis=0, keepdims=True), sinks[kh])
        p = jnp.exp(s - m)
        denoms[kh, qi] = jnp.sum(p, axis=0, keepdims=True) + jnp.exp(sinks[kh] - m)
        probs[kh, qi] = p.astype(BF16)
    outs = {}
    for kh, qi in blocks:
        r0 = qi * WINDOW
        vt_win = vt_all[kh * ATT_HEAD_DIM:(kh + 1) * ATT_HEAD_DIM, r0:r0 + 2 * WINDOW]
        outs[kh, qi] = (_dot(vt_win, probs[kh, qi]) / denoms[kh, qi]).astype(BF16)
    a_t = jnp.concatenate(
        [jnp.concatenate([outs[hd // ATT_GROUP, qi][:, (hd % ATT_GROUP) * WINDOW:(hd % ATT_GROUP + 1) * WINDOW]
                          for qi in range(tm // WINDOW)], axis=1)
         for hd in range(ATT_HEADS)], axis=0)

    u_ext = jnp.concatenate([u_hist, u], axis=0)
    first_rows = (lax.broadcasted_iota(jnp.int32, (POOL_HISTORY, POOL_GROUP), 0) + 1).astype(F32)
    pooled = []
    for gi, w in enumerate(POOL_WINDOWS):
        z = u_ext[:, gi * POOL_GROUP:(gi + 1) * POOL_GROUP]
        shift = 1
        while shift < w:
            z = z + pltpu.roll(z, shift, axis=0)
            shift *= 2
        head_scale = jnp.where(j == 0, 1.0 / jnp.minimum(first_rows, float(w)), 1.0 / w)
        mean = jnp.concatenate([z[POOL_HISTORY:2 * POOL_HISTORY, :] * head_scale,
                                z[2 * POOL_HISTORY:, :] * (1.0 / w)], axis=0)
        pooled.append((mean - u[:, gi * POOL_GROUP:(gi + 1) * POOL_GROUP]).astype(BF16))
    pool_out = jnp.concatenate(
        [_dot(pooled[gi], wpool_ref[gi]) for gi in range(len(POOL_WINDOWS))], axis=1).astype(BF16)

    o_ref[...] = (x + _dot_tn(a_t, w_out_ref[:ATT_WIDTH, :])
                  + _dot(pool_out, w_out_ref[ATT_WIDTH:, :]))
    kprev_ref[...] = kr[tm - WINDOW:, :]
    vtprev_ref[...] = vt_cur[:, tm - WINDOW:]
    uhist_ref[...] = u[tm - POOL_HISTORY:, :]


def _attention_bias():
    c = np.arange(2 * WINDOW)[:, None]
    r = np.arange(ATT_GROUP * WINDOW)[None, :] % WINDOW
    band = (c > r) & (c <= r + WINDOW)
    first = band & (c >= WINDOW)
    return jnp.asarray(np.where(np.stack([band, first]), 0.0, MASKED_SCORE), F32)


def _attn_pool_mixer(x3, gain, w_in, w_out, q_gain, k_gain, sinks, pool_w, pool_scale, cos_t, sin_t):
    b, t, d = x3.shape
    tm = min(AB_ROW_TILE, t)
    ab_in = w_in.shape[1]
    head_of_lane = np.arange(KV_WIDTH) // ATT_HEAD_DIM
    bd = jnp.asarray((head_of_lane[:, None] == head_of_lane[None, :]) / ATT_HEAD_DIM, BF16)
    n_groups = len(POOL_WINDOWS)
    wpool = (pool_w * pool_scale.reshape(n_groups, 1, POOL_GROUP)).astype(BF16)
    half = ATT_HEAD_DIM // 2

    def gain_rows(g, scale):
        partner = jnp.concatenate([g[half:], g[:half]])
        return jnp.stack([jnp.tile(g, ATT_KV_HEADS), jnp.tile(partner, ATT_KV_HEADS)]) * scale

    qg = gain_rows(q_gain, ATT_HEAD_DIM ** -0.5)
    kg = gain_rows(k_gain, 1.0)
    sink_lanes = jnp.repeat(sinks, WINDOW).reshape(ATT_KV_HEADS, ATT_GROUP * WINDOW)
    return pl.pallas_call(
        _ab_kernel,
        out_shape=jax.ShapeDtypeStruct((b, t, d), F32),
        grid=(b, t // tm),
        in_specs=[
            pl.BlockSpec((None, tm, d), lambda i, j: (i, j, 0)),
            _const_spec((d, ab_in)),
            pl.BlockSpec((tm, KV_WIDTH), lambda i, j: (j, 0)),
            pl.BlockSpec((tm, KV_WIDTH), lambda i, j: (j, 0)),
            _const_spec((2, KV_WIDTH)),
            _const_spec((2, KV_WIDTH)),
            _const_spec((KV_WIDTH, KV_WIDTH)),
            _const_spec((2, 2 * WINDOW, ATT_GROUP * WINDOW)),
            _const_spec((ATT_KV_HEADS, ATT_GROUP * WINDOW)),
            _const_spec((len(POOL_WINDOWS), POOL_GROUP, POOL_GROUP)),
            _const_spec((ATT_WIDTH + POOL_WIDTH, d)),
        ],
        out_specs=pl.BlockSpec((None, tm, d), lambda i, j: (i, j, 0)),
        scratch_shapes=[
            pltpu.VMEM((WINDOW, KV_WIDTH), BF16),
            pltpu.VMEM((KV_WIDTH, WINDOW), BF16),
            pltpu.VMEM((POOL_HISTORY, POOL_WIDTH), F32),
        ],
        compiler_params=pltpu.CompilerParams(
            dimension_semantics=("arbitrary", "arbitrary"), vmem_limit_bytes=VMEM_LIMIT_BYTES),
        name="attn_pool_mixer",
    )(x3, _fold_rows(gain, w_in), cos_t, sin_t, qg, kg, bd, _attention_bias(),
      sink_lanes, wpool, w_out.astype(BF16))


def _split_bf16(a):
    hi = a.astype(BF16)
    lo = (a - hi.astype(F32)).astype(BF16)
    return hi, lo


def _node_reference(g, node, row):
    n_rows, width = g.shape
    g3 = g.reshape(n_rows // node, node, width)
    return jnp.broadcast_to(g3[:, row:row + 1, :], g3.shape).reshape(n_rows, width)


def _node_halves(a, node):
    n_rows, width = a.shape
    a3 = a.reshape(n_rows // node, node, width)
    return a3[:, :node // 2, :], a3[:, node // 2:, :]


def _hgrn_node_sizes():
    sizes, n = [], 2 * HGRN_BASE_BLOCK
    while n <= HGRN_CHUNK:
        sizes.append(n)
        n *= 2
    return sizes


def _hgrn_level_codes():
    t = np.arange(HGRN_CHUNK)[:, None]
    s = np.arange(HGRN_CHUNK)[None, :]
    code = np.full((HGRN_CHUNK, HGRN_CHUNK), 1 << 20, np.int32)
    for k, n in reversed(list(enumerate(_hgrn_node_sizes(), start=1))):
        code[(t // n) == (s // n)] = k
    same_base = (t // HGRN_BASE_BLOCK) == (s // HGRN_BASE_BLOCK)
    code[same_base] = np.where(s <= t, 0, 1 << 20)[same_base]
    return jnp.asarray(code), jnp.asarray(s <= t, BF16)


def _hgrn_kernel(layer, x_ref, w_in_ref, lb_logits_ref, w_out_ref, level_ref, tri_ref, o_ref,
                 state_ref):
    tm = x_ref.shape[0]
    dk = HGRN_DK
    cl = HGRN_CHUNK
    j = pl.program_id(1)

    @pl.when(j == 0)
    def _():
        state_ref[...] = jnp.zeros_like(state_ref)

    x = x_ref[...]
    h = _rms_normalize(x).astype(BF16)

    logits = lb_logits_ref[...]
    e = jnp.exp(logits - jnp.max(logits, axis=0, keepdims=True))
    prob = e / jnp.sum(e, axis=0, keepdims=True)
    lb = jnp.sum(prob[:layer + 1, :], axis=0, keepdims=True) - prob[0:1, :]

    tri = tri_ref[...]
    tri2 = jnp.concatenate([tri, tri], axis=1)
    level = level_ref[...]
    base = HGRN_BASE_BLOCK
    diag_mask = level == 0
    node_sizes = _hgrn_node_sizes()
    node_masks = [level <= k + 1 for k in range(len(node_sizes))]
    lane_mean = jnp.full((dk, dk), 1.0 / dk, BF16)

    chunks = range(tm // cl)

    def rows(a, c):
        return a[c * cl:(c + 1) * cl]

    def stage_project(hd, _):
        return _dot(h, w_in_ref[hd])

    def stage_gates(hd, proj):
        qh = proj[:, :dk]
        lb_h = lb[:, hd * dk:(hd + 1) * dk]
        f = lb_h + (1.0 - lb_h) * _sigmoid(proj[:, dk:2 * dk])
        return dict(qf=qh * _sigmoid(qh), key=1.0 - f,
                    lf=_split_bf16(jnp.log2(jnp.maximum(f, GATE_EPS))),
                    v=proj[:, 2 * dk:3 * dk].astype(BF16), gate=_sigmoid(proj[:, 3 * dk:]))

    def stage_cumsum(hd, c, w):
        w["g", c] = _dot(tri2, jnp.concatenate([rows(w["lf"][0], c), rows(w["lf"][1], c)], axis=0))

    def stage_operands(hd, c, w):
        g, qf, key = w["g", c], rows(w["qf"], c), rows(w["key"], c)
        ref0 = _node_reference(g, base, base // 2 - 1)
        ops = [((qf * jnp.exp2(g - ref0)).astype(BF16), (key * jnp.exp2(ref0 - g)).astype(BF16))]
        for n in node_sizes:
            g_l, g_r = _node_halves(g, n)
            ref = g_l[:, n // 2 - 1:n // 2, :]
            qt_r = (_node_halves(qf, n)[1] * jnp.exp2(g_r - ref)).astype(BF16)
            kt_l = (_node_halves(key, n)[0] * jnp.exp2(ref - g_l)).astype(BF16)
            zeros = jnp.zeros_like(qt_r)
            ops.append((jnp.concatenate([zeros, qt_r], axis=1).reshape(cl, dk),
                        jnp.concatenate([kt_l, zeros], axis=1).reshape(cl, dk)))
        g_last = g[cl - 1:cl, :]
        w["levels", c] = ops
        w["q_head", c] = (qf * jnp.exp2(g)).astype(BF16)
        w["k_tail", c] = (key * jnp.exp2(g_last - g)).astype(BF16)
        w["decay", c] = jnp.exp2(g_last)

    def stage_scores(hd, c, w):
        ops = w["levels", c]
        a = jnp.where(diag_mask, _dot_nt(*ops[0]), 0.0)
        for n, nmask, (qt, kt) in zip(node_sizes, node_masks, ops[1:]):
            lvl = _dot_nt(qt, kt)
            a = a + (lvl if n == cl else jnp.where(nmask, lvl, 0.0))
        w["scores", c] = a.astype(BF16)

    def stage_recurrence(hd, c, w):
        st = states[hd]
        vv = rows(w["v"], c)
        w["out", c] = _dot(w["scores", c], vv) + _dot_nt(w["q_head", c], st.astype(BF16))
        states[hd] = st * w["decay", c] + _dot_tn(vv, w["k_tail", c])

    def gate_and_normalize(c, work):
        gated = [w["out", c] * rows(w["gate"], c) for w in work]
        squares = jnp.concatenate([(og * og).astype(BF16) for og in gated], axis=0)
        mean_sq = _dot(squares, lane_mean)
        for hd, w in enumerate(work):
            w["y", c] = (gated[hd] * lax.rsqrt(rows(mean_sq, hd) + NORM_EPS)).astype(BF16)

    heads = range(HGRN_HEADS)
    states = [state_ref[hd] for hd in heads]
    work = [stage_gates(hd, proj) for hd, proj in enumerate([stage_project(hd, None) for hd in heads])]
    for stage in (stage_cumsum, stage_operands, stage_scores, stage_recurrence):
        for c in chunks:
            for hd in heads:
                stage(hd, c, work[hd])
    for c in chunks:
        gate_and_normalize(c, work)

    for hd in heads:
        state_ref[hd] = states[hd]
    y = jnp.concatenate(
        [jnp.concatenate([work[hd]["y", c] for c in chunks], axis=0) for hd in heads], axis=1)
    o_ref[...] = x + _dot(y, w_out_ref[...])


def _hgrn_mixer(x3, gain, w_in, w_out, out_gain, lb_logits, layer):
    b, t, d = x3.shape
    tm = min(HGRN_ROW_TILE, t)
    n_layers, kw = lb_logits.shape
    w_heads = _fold_rows(gain, w_in).reshape(d, 4, HGRN_HEADS, HGRN_DK).transpose(2, 0, 1, 3).reshape(
        HGRN_HEADS, d, 4 * HGRN_DK)
    return pl.pallas_call(
        functools.partial(_hgrn_kernel, layer),
        out_shape=jax.ShapeDtypeStruct((b, t, d), F32),
        grid=(b, t // tm),
        in_specs=[
            pl.BlockSpec((None, tm, d), lambda i, j: (i, j, 0)),
            _const_spec((HGRN_HEADS, d, 4 * HGRN_DK)),
            _const_spec((n_layers, kw)),
            _const_spec((kw, d)),
            _const_spec((HGRN_CHUNK, HGRN_CHUNK)),
            _const_spec((HGRN_CHUNK, HGRN_CHUNK)),
        ],
        out_specs=pl.BlockSpec((None, tm, d), lambda i, j: (i, j, 0)),
        scratch_shapes=[
            pltpu.VMEM((HGRN_HEADS, HGRN_DK, HGRN_DK), F32),
        ],
        compiler_params=pltpu.CompilerParams(
            dimension_semantics=("arbitrary", "arbitrary"), vmem_limit_bytes=VMEM_LIMIT_BYTES),
        name="hgrn_mixer",
    )(x3, w_heads, lb_logits, _fold_rows(jnp.tile(out_gain, HGRN_HEADS), w_out), *_hgrn_level_codes())


def kernel(x, positions, norm_gains, ffn_w_gate, ffn_w_up, ffn_w_down, ab_w_in, ab_w_out, q_norm_gain, k_norm_gain, attn_sinks, pool_w, pool_scale, c_w_in, c_w_out, c_out_norm_gain, lb_logits):
    b, t, d = x.shape
    depth = norm_gains.shape[0]
    cos_t, sin_t = _rope_tables(positions)

    ffn_gain = jnp.stack([norm_gains[:, 0], norm_gains[:, 2]], axis=1)[..., None]
    wg_all = (ffn_gain * ffn_w_gate).astype(BF16)
    wu_all = (ffn_gain * ffn_w_up).astype(BF16)
    wd_all = (0.5 * ffn_w_down).astype(BF16)

    def ffn(x3, layer, which):
        return _ffn(x3.reshape(b * t, d), wg_all, wu_all, wd_all, layer, which).reshape(b, t, d)

    for layer in range(depth):
        x = ffn(x, layer, 0)
        jx = layer // 2
        if layer % 2 == 0:
            x = _attn_pool_mixer(x, norm_gains[layer, 1], ab_w_in[jx], ab_w_out[jx], q_norm_gain[jx],
                                 k_norm_gain[jx], attn_sinks[jx], pool_w[jx], pool_scale[jx], cos_t, sin_t)
        else:
            x = _hgrn_mixer(x, norm_gains[layer, 1], c_w_in[jx], c_w_out[jx], c_out_norm_gain[jx],
                            lb_logits, jx)
        x = ffn(x, layer, 1)
    return x
```

```python
import functools

import jax
import jax.numpy as jnp
import numpy as np
from jax import lax
from jax.experimental import pallas as pl
from jax.experimental.pallas import tpu as pltpu

F32 = jnp.float32
BF16 = jnp.bfloat16

NORM_EPS = 1e-6
GATE_EPS = 1e-6
ROPE_THETA = 10000.0

ATT_HEADS = 8
ATT_KV_HEADS = 2
ATT_GROUP = ATT_HEADS // ATT_KV_HEADS
ATT_HEAD_DIM = 64
WINDOW = 128
ATT_WIDTH = ATT_HEADS * ATT_HEAD_DIM
KV_WIDTH = ATT_KV_HEADS * ATT_HEAD_DIM
POOL_WINDOWS = (2, 4, 8, 16)
POOL_GROUP = 128
POOL_WIDTH = POOL_GROUP * len(POOL_WINDOWS)
POOL_HISTORY = 16
HGRN_HEADS = 8
HGRN_DK = 128

FFN_ROW_TILE = 1024
FFN_COL_CHUNK = 256
AB_ROW_TILE = 1024
HGRN_ROW_TILE = 1024
HGRN_CHUNK = 128
HGRN_BASE_BLOCK = 8
VMEM_LIMIT_BYTES = 56 * 1024 * 1024
MASKED_SCORE = -1e30


def _rms_normalize(x):
    ms = jnp.mean(x * x, axis=-1, keepdims=True)
    return x * lax.rsqrt(ms + NORM_EPS)


def _fold_rows(gain, w):
    return (gain[:, None] * w).astype(BF16)


def _sigmoid(x):
    return 0.5 * jnp.tanh(0.5 * x) + 0.5


def _const_spec(shape):
    nd = len(shape)
    return pl.BlockSpec(shape, lambda *_: (0,) * nd, pipeline_mode=pl.Buffered(1))


def _dot(a, b):
    return jnp.dot(a, b, preferred_element_type=F32)


def _dot_nt(a, b):
    return lax.dot_general(a, b, (((1,), (1,)), ((), ())), preferred_element_type=F32)


def _dot_tn(a, b):
    return lax.dot_general(a, b, (((0,), (0,)), ((), ())), preferred_element_type=F32)


def _ffn_kernel(x_ref, wg_ref, wu_ref, wd_ref, o_ref):
    x = x_ref[...]
    h = _rms_normalize(x).astype(BF16)
    d_ff = wg_ref.shape[1]
    acc = x
    for c in range(d_ff // FFN_COL_CHUNK):
        sl = slice(c * FFN_COL_CHUNK, (c + 1) * FFN_COL_CHUNK)
        g = _dot(h, wg_ref[:, sl])
        u = _dot(h, wu_ref[:, sl])
        a = (g * jax.nn.sigmoid(g) * u).astype(BF16)
        acc = acc + _dot(a, wd_ref[sl, :])
    o_ref[...] = acc


def _ffn(x2, wg, wu, wd, layer, which):
    n, d = x2.shape
    d_ff = wg.shape[-1]
    tm = min(FFN_ROW_TILE, n)

    def weight_spec(rows, cols):
        return pl.BlockSpec((None, None, rows, cols), lambda i: (layer, which, 0, 0),
                            pipeline_mode=pl.Buffered(1))

    return pl.pallas_call(
        _ffn_kernel,
        out_shape=jax.ShapeDtypeStruct((n, d), F32),
        grid=(n // tm,),
        in_specs=[
            pl.BlockSpec((tm, d), lambda i: (i, 0)),
            weight_spec(d, d_ff),
            weight_spec(d, d_ff),
            weight_spec(d_ff, d),
        ],
        out_specs=pl.BlockSpec((tm, d), lambda i: (i, 0)),
        compiler_params=pltpu.CompilerParams(
            dimension_semantics=("arbitrary",), vmem_limit_bytes=VMEM_LIMIT_BYTES),
        name="ffn",
    )(x2, wg, wu, wd)


def _rope_table_kernel(pos_ref, inv_freq_ref, sign_ref, cos_ref, sin_ref):
    ang = pos_ref[...].astype(F32) * inv_freq_ref[...]
    cos_ref[...] = jnp.cos(ang)
    sin_ref[...] = jnp.sin(ang) * sign_ref[...]


def _rope_tables(positions):
    t = positions.shape[0]
    half = ATT_HEAD_DIM // 2
    lane = np.arange(KV_WIDTH)
    inv_freq = ROPE_THETA ** (-jnp.arange(half, dtype=F32) / half)
    inv_freq_lanes = jnp.tile(inv_freq, KV_WIDTH // half).reshape(1, KV_WIDTH)
    sign = jnp.asarray(np.where(lane % ATT_HEAD_DIM < half, -1.0, 1.0), F32).reshape(1, KV_WIDTH)
    tr = min(256, t)
    return pl.pallas_call(
        _rope_table_kernel,
        out_shape=(jax.ShapeDtypeStruct((t, KV_WIDTH), F32),) * 2,
        grid=(t // tr,),
        in_specs=[pl.BlockSpec((tr, 1), lambda i: (i, 0)),
                  _const_spec((1, KV_WIDTH)), _const_spec((1, KV_WIDTH))],
        out_specs=(pl.BlockSpec((tr, KV_WIDTH), lambda i: (i, 0)),) * 2,
        compiler_params=pltpu.CompilerParams(dimension_semantics=("arbitrary",)),
        name="rope_tables",
    )(positions.reshape(t, 1), inv_freq_lanes, sign)


def _rotate_half_partner(z):
    w = z.shape[1]
    lane = lax.broadcasted_iota(jnp.int32, z.shape, 1)
    first_half = (lane & (ATT_HEAD_DIM // 2)) == 0
    return jnp.where(first_half, pltpu.roll(z, w - ATT_HEAD_DIM // 2, axis=1),
                     pltpu.roll(z, ATT_HEAD_DIM // 2, axis=1))


def _ab_kernel(x_ref, w_in_ref, cos_ref, sin_ref, qg_ref, kg_ref, bd_ref, bias_ref,
               sink_ref, wpool_ref, w_out_ref, o_ref,
               kprev_ref, vtprev_ref, uhist_ref):
    tm = x_ref.shape[0]
    j = pl.program_id(1)

    @pl.when(j == 0)
    def _():
        kprev_ref[...] = jnp.zeros_like(kprev_ref)
        vtprev_ref[...] = jnp.zeros_like(vtprev_ref)
        uhist_ref[...] = jnp.zeros_like(uhist_ref)

    k_prev = kprev_ref[...]
    vt_prev = vtprev_ref[...]
    u_hist = uhist_ref[...]
    bias_rest = bias_ref[0]
    bias_first = bias_ref[jnp.where(j == 0, 1, 0)]

    x = x_ref[...]
    h = _rms_normalize(x).astype(BF16)
    proj = _dot(h, w_in_ref[...])
    q = proj[:, :ATT_WIDTH]
    k = proj[:, ATT_WIDTH:ATT_WIDTH + KV_WIDTH]
    v = proj[:, ATT_WIDTH + KV_WIDTH:ATT_WIDTH + 2 * KV_WIDTH]
    u = proj[:, ATT_WIDTH + 2 * KV_WIDTH:]

    bd = bd_ref[...]
    q_sq = (q * q).astype(BF16)
    q_ms = jnp.concatenate(
        [_dot(q_sq[:, g * KV_WIDTH:(g + 1) * KV_WIDTH], bd) for g in range(ATT_WIDTH // KV_WIDTH)], axis=1)
    k_ms = _dot((k * k).astype(BF16), bd)
    cos_kv = cos_ref[...]
    sin_kv = sin_ref[...]
    groups = ATT_WIDTH // KV_WIDTH
    q_cos = jnp.concatenate([cos_kv * qg_ref[0:1, :]] * groups, axis=1)
    q_sin = jnp.concatenate([sin_kv * qg_ref[1:2, :]] * groups, axis=1)
    qr = ((q * q_cos + _rotate_half_partner(q) * q_sin) * lax.rsqrt(q_ms + NORM_EPS)).astype(BF16)
    kr = ((k * (cos_kv * kg_ref[0:1, :]) + _rotate_half_partner(k) * (sin_kv * kg_ref[1:2, :]))
          * lax.rsqrt(k_ms + NORM_EPS)).astype(BF16)
    k_all = jnp.concatenate([k_prev, kr], axis=0)
    vt_cur = v.T.astype(BF16)
    vt_all = jnp.concatenate([vt_prev, vt_cur], axis=1)

    blocks = [(kh, qi) for kh in range(ATT_KV_HEADS) for qi in range(tm // WINDOW)]
    sinks = [sink_ref[kh:kh + 1, :] for kh in range(ATT_KV_HEADS)]
    scores = {}
    for kh, qi in blocks:
        r0 = qi * WINDOW
        qs = jnp.concatenate(
            [qr[r0:r0 + WINDOW, (kh * ATT_GROUP + g) * ATT_HEAD_DIM:(kh * ATT_GROUP + g + 1) * ATT_HEAD_DIM]
             for g in range(ATT_GROUP)], axis=0)
        k_win = k_all[r0:r0 + 2 * WINDOW, kh * ATT_HEAD_DIM:(kh + 1) * ATT_HEAD_DIM]
        scores[kh, qi] = _dot_nt(k_win, qs) + (bias_first if qi == 0 else bias_rest)
    probs, denoms = {}, {}
    for kh, qi in blocks:
        s = scores[kh, qi]
        m = jnp.maximum(jnp.max(s, axis=0, keepdims=True), sinks[kh])
        p = jnp.exp(s - m)
        denoms[kh, qi] = jnp.sum(p, axis=0, keepdims=True) + jnp.exp(sinks[kh] - m)
        probs[kh, qi] = p.astype(BF16)
    outs = {}
    for kh, qi in blocks:
        r0 = qi * WINDOW
        vt_win = vt_all[kh * ATT_HEAD_DIM:(kh + 1) * ATT_HEAD_DIM, r0:r0 + 2 * WINDOW]
        outs[kh, qi] = (_dot(vt_win, probs[kh, qi]) / denoms[kh, qi]).astype(BF16)
    a_t = jnp.concatenate(
        [jnp.concatenate([outs[hd // ATT_GROUP, qi][:, (hd % ATT_GROUP) * WINDOW:(hd % ATT_GROUP + 1) * WINDOW]
                          for qi in range(tm // WINDOW)], axis=1)
         for hd in range(ATT_HEADS)], axis=0)

    u_ext = jnp.concatenate([u_hist, u], axis=0)
    first_rows = (lax.broadcasted_iota(jnp.int32, (POOL_HISTORY, POOL_GROUP), 0) + 1).astype(F32)
    pooled = []
    for gi, w in enumerate(POOL_WINDOWS):
        z = u_ext[:, gi * POOL_GROUP:(gi + 1) * POOL_GROUP]
        shift = 1
        while shift < w:
            z = z + pltpu.roll(z, shift, axis=0)
            shift *= 2
        head_scale = jnp.where(j == 0, 1.0 / jnp.minimum(first_rows, float(w)), 1.0 / w)
        mean = jnp.concatenate([z[POOL_HISTORY:2 * POOL_HISTORY, :] * head_scale,
                                z[2 * POOL_HISTORY:, :] * (1.0 / w)], axis=0)
        pooled.append((mean - u[:, gi * POOL_GROUP:(gi + 1) * POOL_GROUP]).astype(BF16))
    pool_out = jnp.concatenate(
        [_dot(pooled[gi], wpool_ref[gi]) for gi in range(len(POOL_WINDOWS))], axis=1).astype(BF16)

    o_ref[...] = (x + _dot_tn(a_t, w_out_ref[:ATT_WIDTH, :])
                  + _dot(pool_out, w_out_ref[ATT_WIDTH:, :]))
    kprev_ref[...] = kr[tm - WINDOW:, :]
    vtprev_ref[...] = vt_cur[:, tm - WINDOW:]
    uhist_ref[...] = u[tm - POOL_HISTORY:, :]


def _attention_bias():
    c = np.arange(2 * WINDOW)[:, None]
    r = np.arange(ATT_GROUP * WINDOW)[None, :] % WINDOW
    band = (c > r) & (c <= r + WINDOW)
    first = band & (c >= WINDOW)
    return jnp.asarray(np.where(np.stack([band, first]), 0.0, MASKED_SCORE), F32)


def _attn_pool_mixer(x3, gain, w_in, w_out, q_gain, k_gain, sinks, pool_w, pool_scale, cos_t, sin_t):
    b, t, d = x3.shape
    tm = min(AB_ROW_TILE, t)
    ab_in = w_in.shape[1]
    head_of_lane = np.arange(KV_WIDTH) // ATT_HEAD_DIM
    bd = jnp.asarray((head_of_lane[:, None] == head_of_lane[None, :]) / ATT_HEAD_DIM, BF16)
    n_groups = len(POOL_WINDOWS)
    wpool = (pool_w * pool_scale.reshape(n_groups, 1, POOL_GROUP)).astype(BF16)
    half = ATT_HEAD_DIM // 2

    def gain_rows(g, scale):
        partner = jnp.concatenate([g[half:], g[:half]])
        return jnp.stack([jnp.tile(g, ATT_KV_HEADS), jnp.tile(partner, ATT_KV_HEADS)]) * scale

    qg = gain_rows(q_gain, ATT_HEAD_DIM ** -0.5)
    kg = gain_rows(k_gain, 1.0)
    sink_lanes = jnp.repeat(sinks, WINDOW).reshape(ATT_KV_HEADS, ATT_GROUP * WINDOW)
    return pl.pallas_call(
        _ab_kernel,
        out_shape=jax.ShapeDtypeStruct((b, t, d), F32),
        grid=(b, t // tm),
        in_specs=[
            pl.BlockSpec((None, tm, d), lambda i, j: (i, j, 0)),
            _const_spec((d, ab_in)),
            pl.BlockSpec((tm, KV_WIDTH), lambda i, j: (j, 0)),
            pl.BlockSpec((tm, KV_WIDTH), lambda i, j: (j, 0)),
            _const_spec((2, KV_WIDTH)),
            _const_spec((2, KV_WIDTH)),
            _const_spec((KV_WIDTH, KV_WIDTH)),
            _const_spec((2, 2 * WINDOW, ATT_GROUP * WINDOW)),
            _const_spec((ATT_KV_HEADS, ATT_GROUP * WINDOW)),
            _const_spec((len(POOL_WINDOWS), POOL_GROUP, POOL_GROUP)),
            _const_spec((ATT_WIDTH + POOL_WIDTH, d)),
        ],
        out_specs=pl.BlockSpec((None, tm, d), lambda i, j: (i, j, 0)),
        scratch_shapes=[
            pltpu.VMEM((WINDOW, KV_WIDTH), BF16),
            pltpu.VMEM((KV_WIDTH, WINDOW), BF16),
            pltpu.VMEM((POOL_HISTORY, POOL_WIDTH), F32),
        ],
        compiler_params=pltpu.CompilerParams(
            dimension_semantics=("arbitrary", "arbitrary"), vmem_limit_bytes=VMEM_LIMIT_BYTES),
        name="attn_pool_mixer",
    )(x3, _fold_rows(gain, w_in), cos_t, sin_t, qg, kg, bd, _attention_bias(),
      sink_lanes, wpool, w_out.astype(BF16))


def _split_bf16(a):
    hi = a.astype(BF16)
    lo = (a - hi.astype(F32)).astype(BF16)
    return hi, lo


def _node_reference(g, node, row):
    n_rows, width = g.shape
    g3 = g.reshape(n_rows // node, node, width)
    return jnp.broadcast_to(g3[:, row:row + 1, :], g3.shape).reshape(n_rows, width)


def _node_halves(a, node):
    n_rows, width = a.shape
    a3 = a.reshape(n_rows // node, node, width)
    return a3[:, :node // 2, :], a3[:, node // 2:, :]


def _hgrn_node_sizes():
    sizes, n = [], 2 * HGRN_BASE_BLOCK
    while n <= HGRN_CHUNK:
        sizes.append(n)
        n *= 2
    return sizes


def _hgrn_level_codes():
    t = np.arange(HGRN_CHUNK)[:, None]
    s = np.arange(HGRN_CHUNK)[None, :]
    code = np.full((HGRN_CHUNK, HGRN_CHUNK), 1 << 20, np.int32)
    for k, n in reversed(list(enumerate(_hgrn_node_sizes(), start=1))):
        code[(t // n) == (s // n)] = k
    same_base = (t // HGRN_BASE_BLOCK) == (s // HGRN_BASE_BLOCK)
    code[same_base] = np.where(s <= t, 0, 1 << 20)[same_base]
    return jnp.asarray(code), jnp.asarray(s <= t, BF16)


def _hgrn_kernel(layer, x_ref, w_in_ref, lb_logits_ref, w_out_ref, level_ref, tri_ref, o_ref,
                 state_ref):
    tm = x_ref.shape[0]
    dk = HGRN_DK
    cl = HGRN_CHUNK
    j = pl.program_id(1)

    @pl.when(j == 0)
    def _():
        state_ref[...] = jnp.zeros_like(state_ref)

    x = x_ref[...]
    h = _rms_normalize(x).astype(BF16)

    logits = lb_logits_ref[...]
    e = jnp.exp(logits - jnp.max(logits, axis=0, keepdims=True))
    prob = e / jnp.sum(e, axis=0, keepdims=True)
    lb = jnp.sum(prob[:layer + 1, :], axis=0, keepdims=True) - prob[0:1, :]

    tri = tri_ref[...]
    tri2 = jnp.concatenate([tri, tri], axis=1)
    level = level_ref[...]
    base = HGRN_BASE_BLOCK
    diag_mask = level == 0
    node_sizes = _hgrn_node_sizes()
    node_masks = [level <= k + 1 for k in range(len(node_sizes))]
    lane_mean = jnp.full((dk, dk), 1.0 / dk, BF16)

    chunks = range(tm // cl)

    def rows(a, c):
        return a[c * cl:(c + 1) * cl]

    def stage_project(hd, _):
        return _dot(h, w_in_ref[hd])

    def stage_gates(hd, proj):
        qh = proj[:, :dk]
        lb_h = lb[:, hd * dk:(hd + 1) * dk]
        f = lb_h + (1.0 - lb_h) * _sigmoid(proj[:, dk:2 * dk])
        return dict(qf=qh * _sigmoid(qh), key=1.0 - f,
                    lf=_split_bf16(jnp.log2(jnp.maximum(f, GATE_EPS))),
                    v=proj[:, 2 * dk:3 * dk].astype(BF16), gate=_sigmoid(proj[:, 3 * dk:]))

    def stage_cumsum(hd, c, w):
        w["g", c] = _dot(tri2, jnp.concatenate([rows(w["lf"][0], c), rows(w["lf"][1], c)], axis=0))

    def stage_operands(hd, c, w):
        g, qf, key = w["g", c], rows(w["qf"], c), rows(w["key"], c)
        ref0 = _node_reference(g, base, base // 2 - 1)
        ops = [((qf * jnp.exp2(g - ref0)).astype(BF16), (key * jnp.exp2(ref0 - g)).astype(BF16))]
        for n in node_sizes:
            g_l, g_r = _node_halves(g, n)
            ref = g_l[:, n // 2 - 1:n // 2, :]
            qt_r = (_node_halves(qf, n)[1] * jnp.exp2(g_r - ref)).astype(BF16)
            kt_l = (_node_halves(key, n)[0] * jnp.exp2(ref - g_l)).astype(BF16)
            zeros = jnp.zeros_like(qt_r)
            ops.append((jnp.concatenate([zeros, qt_r], axis=1).reshape(cl, dk),
                        jnp.concatenate([kt_l, zeros], axis=1).reshape(cl, dk)))
        g_last = g[cl - 1:cl, :]
        w["levels", c] = ops
        w["q_head", c] = (qf * jnp.exp2(g)).astype(BF16)
        w["k_tail", c] = (key * jnp.exp2(g_last - g)).astype(BF16)
        w["decay", c] = jnp.exp2(g_last)

    def stage_scores(hd, c, w):
        ops = w["levels", c]
        a = jnp.where(diag_mask, _dot_nt(*ops[0]), 0.0)
        for n, nmask, (qt, kt) in zip(node_sizes, node_masks, ops[1:]):
            lvl = _dot_nt(qt, kt)
            a = a + (lvl if n == cl else jnp.where(nmask, lvl, 0.0))
        w["scores", c] = a.astype(BF16)

    def stage_recurrence(hd, c, w):
        st = states[hd]
        vv = rows(w["v"], c)
        w["out", c] = _dot(w["scores", c], vv) + _dot_nt(w["q_head", c], st.astype(BF16))
        states[hd] = st * w["decay", c] + _dot_tn(vv, w["k_tail", c])

    def gate_and_normalize(c, work):
        gated = [w["out", c] * rows(w["gate"], c) for w in work]
        squares = jnp.concatenate([(og * og).astype(BF16) for og in gated], axis=0)
        mean_sq = _dot(squares, lane_mean)
        for hd, w in enumerate(work):
            w["y", c] = (gated[hd] * lax.rsqrt(rows(mean_sq, hd) + NORM_EPS)).astype(BF16)

    heads = range(HGRN_HEADS)
    states = [state_ref[hd] for hd in heads]
    work = [stage_gates(hd, proj) for hd, proj in enumerate([stage_project(hd, None) for hd in heads])]
    for stage in (stage_cumsum, stage_operands, stage_scores, stage_recurrence):
        for c in chunks:
            for hd in heads:
                stage(hd, c, work[hd])
    for c in chunks:
        gate_and_normalize(c, work)

    for hd in heads:
        state_ref[hd] = states[hd]
    y = jnp.concatenate(
        [jnp.concatenate([work[hd]["y", c] for c in chunks], axis=0) for hd in heads], axis=1)
    o_ref[...] = x + _dot(y, w_out_ref[...])


def _hgrn_mixer(x3, gain, w_in, w_out, out_gain, lb_logits, layer):
    b, t, d = x3.shape
    tm = min(HGRN_ROW_TILE, t)
    n_layers, kw = lb_logits.shape
    w_heads = _fold_rows(gain, w_in).reshape(d, 4, HGRN_HEADS, HGRN_DK).transpose(2, 0, 1, 3).reshape(
        HGRN_HEADS, d, 4 * HGRN_DK)
    return pl.pallas_call(
        functools.partial(_hgrn_kernel, layer),
        out_shape=jax.ShapeDtypeStruct((b, t, d), F32),
        grid=(b, t // tm),
        in_specs=[
            pl.BlockSpec((None, tm, d), lambda i, j: (i, j, 0)),
            _const_spec((HGRN_HEADS, d, 4 * HGRN_DK)),
            _const_spec((n_layers, kw)),
            _const_spec((kw, d)),
            _const_spec((HGRN_CHUNK, HGRN_CHUNK)),
            _const_spec((HGRN_CHUNK, HGRN_CHUNK)),
        ],
        out_specs=pl.BlockSpec((None, tm, d), lambda i, j: (i, j, 0)),
        scratch_shapes=[
            pltpu.VMEM((HGRN_HEADS, HGRN_DK, HGRN_DK), F32),
        ],
        compiler_params=pltpu.CompilerParams(
            dimension_semantics=("arbitrary", "arbitrary"), vmem_limit_bytes=VMEM_LIMIT_BYTES),
        name="hgrn_mixer",
    )(x3, w_heads, lb_logits, _fold_rows(jnp.tile(out_gain, HGRN_HEADS), w_out), *_hgrn_level_codes())


def kernel(x, positions, norm_gains, ffn_w_gate, ffn_w_up, ffn_w_down, ab_w_in, ab_w_out, q_norm_gain, k_norm_gain, attn_sinks, pool_w, pool_scale, c_w_in, c_w_out, c_out_norm_gain, lb_logits):
    b, t, d = x.shape
    depth = norm_gains.shape[0]
    cos_t, sin_t = _rope_tables(positions)

    ffn_gain = jnp.stack([norm_gains[:, 0], norm_gains[:, 2]], axis=1)[..., None]
    wg_all = (ffn_gain * ffn_w_gate).astype(BF16)
    wu_all = (ffn_gain * ffn_w_up).astype(BF16)
    wd_all = (0.5 * ffn_w_down).astype(BF16)

    def ffn(x3, layer, which):
        return _ffn(x3.reshape(b * t, d), wg_all, wu_all, wd_all, layer, which).reshape(b, t, d)

    for layer in range(depth):
        x = ffn(x, layer, 0)
        jx = layer // 2
        if layer % 2 == 0:
            x = _attn_pool_mixer(x, norm_gains[layer, 1], ab_w_in[jx], ab_w_out[jx], q_norm_gain[jx],
                                 k_norm_gain[jx], attn_sinks[jx], pool_w[jx], pool_scale[jx], cos_t, sin_t)
        else:
            x = _hgrn_mixer(x, norm_gains[layer, 1], c_w_in[jx], c_w_out[jx], c_out_norm_gain[jx],
                            lb_logits, jx)
        x = ffn(x, layer, 1)
    return x
```

```python
import functools

import jax
import jax.numpy as jnp
import numpy as np
from jax import lax
from jax.experimental import pallas as pl
from jax.experimental.pallas import tpu as pltpu

F32 = jnp.float32
BF16 = jnp.bfloat16

NORM_EPS = 1e-6
GATE_EPS = 1e-6
ROPE_THETA = 10000.0

ATT_HEADS = 8
ATT_KV_HEADS = 2
ATT_GROUP = ATT_HEADS // ATT_KV_HEADS
ATT_HEAD_DIM = 64
WINDOW = 128
ATT_WIDTH = ATT_HEADS * ATT_HEAD_DIM
KV_WIDTH = ATT_KV_HEADS * ATT_HEAD_DIM
POOL_WINDOWS = (2, 4, 8, 16)
POOL_GROUP = 128
POOL_WIDTH = POOL_GROUP * len(POOL_WINDOWS)
POOL_HISTORY = 16
HGRN_HEADS = 8
HGRN_DK = 128

FFN_ROW_TILE = 1024
FFN_COL_CHUNK = 256
AB_ROW_TILE = 1024
HGRN_ROW_TILE = 1024
HGRN_CHUNK = 128
HGRN_BASE_BLOCK = 8
HGRN_COMPACT_NODE = 16
VMEM_LIMIT_BYTES = 56 * 1024 * 1024
MASKED_SCORE = -1e30


def _rms_normalize(x):
    ms = jnp.mean(x * x, axis=-1, keepdims=True)
    return x * lax.rsqrt(ms + NORM_EPS)


def _fold_rows(gain, w):
    return (gain[:, None] * w).astype(BF16)


def _sigmoid(x):
    return 0.5 * jnp.tanh(0.5 * x) + 0.5


def _const_spec(shape):
    nd = len(shape)
    return pl.BlockSpec(shape, lambda *_: (0,) * nd, pipeline_mode=pl.Buffered(1))


def _dot(a, b):
    return jnp.dot(a, b, preferred_element_type=F32)


def _dot_nt(a, b):
    return lax.dot_general(a, b, (((1,), (1,)), ((), ())), preferred_element_type=F32)


def _dot_tn(a, b):
    return lax.dot_general(a, b, (((0,), (0,)), ((), ())), preferred_element_type=F32)


def _ffn_kernel(x_ref, wg_ref, wu_ref, wd_ref, o_ref):
    x = x_ref[...]
    h = _rms_normalize(x).astype(BF16)
    d_ff = wg_ref.shape[1]
    acc = x
    for c in range(d_ff // FFN_COL_CHUNK):
        sl = slice(c * FFN_COL_CHUNK, (c + 1) * FFN_COL_CHUNK)
        g = _dot(h, wg_ref[:, sl])
        u = _dot(h, wu_ref[:, sl])
        a = (g * jax.nn.sigmoid(g) * u).astype(BF16)
        acc = acc + _dot(a, wd_ref[sl, :])
    o_ref[...] = acc


def _ffn(x2, wg, wu, wd, layer, which):
    n, d = x2.shape
    d_ff = wg.shape[-1]
    tm = min(FFN_ROW_TILE, n)

    def weight_spec(rows, cols):
        return pl.BlockSpec((None, None, rows, cols), lambda i: (layer, which, 0, 0),
                            pipeline_mode=pl.Buffered(1))

    return pl.pallas_call(
        _ffn_kernel,
        out_shape=jax.ShapeDtypeStruct((n, d), F32),
        grid=(n // tm,),
        in_specs=[
            pl.BlockSpec((tm, d), lambda i: (i, 0)),
            weight_spec(d, d_ff),
            weight_spec(d, d_ff),
            weight_spec(d_ff, d),
        ],
        out_specs=pl.BlockSpec((tm, d), lambda i: (i, 0)),
        compiler_params=pltpu.CompilerParams(
            dimension_semantics=("arbitrary",), vmem_limit_bytes=VMEM_LIMIT_BYTES),
        name="ffn",
    )(x2, wg, wu, wd)


def _rope_table_kernel(pos_ref, inv_freq_ref, sign_ref, cos_ref, sin_ref):
    ang = pos_ref[...].astype(F32) * inv_freq_ref[...]
    cos_ref[...] = jnp.cos(ang)
    sin_ref[...] = jnp.sin(ang) * sign_ref[...]


def _rope_tables(positions):
    t = positions.shape[0]
    half = ATT_HEAD_DIM // 2
    lane = np.arange(KV_WIDTH)
    inv_freq = ROPE_THETA ** (-jnp.arange(half, dtype=F32) / half)
    inv_freq_lanes = jnp.tile(inv_freq, KV_WIDTH // half).reshape(1, KV_WIDTH)
    sign = jnp.asarray(np.where(lane % ATT_HEAD_DIM < half, -1.0, 1.0), F32).reshape(1, KV_WIDTH)
    tr = min(256, t)
    return pl.pallas_call(
        _rope_table_kernel,
        out_shape=(jax.ShapeDtypeStruct((t, KV_WIDTH), F32),) * 2,
        grid=(t // tr,),
        in_specs=[pl.BlockSpec((tr, 1), lambda i: (i, 0)),
                  _const_spec((1, KV_WIDTH)), _const_spec((1, KV_WIDTH))],
        out_specs=(pl.BlockSpec((tr, KV_WIDTH), lambda i: (i, 0)),) * 2,
        compiler_params=pltpu.CompilerParams(dimension_semantics=("arbitrary",)),
        name="rope_tables",
    )(positions.reshape(t, 1), inv_freq_lanes, sign)


def _rotate_half_partner(z):
    w = z.shape[1]
    lane = lax.broadcasted_iota(jnp.int32, z.shape, 1)
    first_half = (lane & (ATT_HEAD_DIM // 2)) == 0
    return jnp.where(first_half, pltpu.roll(z, w - ATT_HEAD_DIM // 2, axis=1),
                     pltpu.roll(z, ATT_HEAD_DIM // 2, axis=1))


def _ab_kernel(x_ref, w_in_ref, cos_ref, sin_ref, qg_ref, kg_ref, bd_ref, bias_ref,
               sink_ref, wpool_ref, w_out_ref, o_ref,
               kprev_ref, vtprev_ref, uhist_ref):
    tm = x_ref.shape[0]
    j = pl.program_id(1)

    @pl.when(j == 0)
    def _():
        kprev_ref[...] = jnp.zeros_like(kprev_ref)
        vtprev_ref[...] = jnp.zeros_like(vtprev_ref)
        uhist_ref[...] = jnp.zeros_like(uhist_ref)

    k_prev = kprev_ref[...]
    vt_prev = vtprev_ref[...]
    u_hist = uhist_ref[...]
    bias_rest = bias_ref[0]
    bias_first = bias_ref[jnp.where(j == 0, 1, 0)]

    x = x_ref[...]
    h = _rms_normalize(x).astype(BF16)
    proj = _dot(h, w_in_ref[...])
    q = proj[:, :ATT_WIDTH]
    k = proj[:, ATT_WIDTH:ATT_WIDTH + KV_WIDTH]
    v = proj[:, ATT_WIDTH + KV_WIDTH:ATT_WIDTH + 2 * KV_WIDTH]
    u = proj[:, ATT_WIDTH + 2 * KV_WIDTH:]

    bd = bd_ref[...]
    q_sq = (q * q).astype(BF16)
    q_ms = jnp.concatenate(
        [_dot(q_sq[:, g * KV_WIDTH:(g + 1) * KV_WIDTH], bd) for g in range(ATT_WIDTH // KV_WIDTH)], axis=1)
    k_ms = _dot((k * k).astype(BF16), bd)
    cos_kv = cos_ref[...]
    sin_kv = sin_ref[...]
    groups = ATT_WIDTH // KV_WIDTH
    q_cos = jnp.concatenate([cos_kv * qg_ref[0:1, :]] * groups, axis=1)
    q_sin = jnp.concatenate([sin_kv * qg_ref[1:2, :]] * groups, axis=1)
    qr = ((q * q_cos + _rotate_half_partner(q) * q_sin) * lax.rsqrt(q_ms + NORM_EPS)).astype(BF16)
    kr = ((k * (cos_kv * kg_ref[0:1, :]) + _rotate_half_partner(k) * (sin_kv * kg_ref[1:2, :]))
          * lax.rsqrt(k_ms + NORM_EPS)).astype(BF16)
    k_all = jnp.concatenate([k_prev, kr], axis=0)
    vt_cur = v.T.astype(BF16)
    vt_all = jnp.concatenate([vt_prev, vt_cur], axis=1)

    blocks = [(kh, qi) for kh in range(ATT_KV_HEADS) for qi in range(tm // WINDOW)]
    sinks = [sink_ref[kh:kh + 1, :] for kh in range(ATT_KV_HEADS)]
    scores = {}
    for kh, qi in blocks:
        r0 = qi * WINDOW
        qs = jnp.concatenate(
            [qr[r0:r0 + WINDOW, (kh * ATT_GROUP + g) * ATT_HEAD_DIM:(kh * ATT_GROUP + g + 1) * ATT_HEAD_DIM]
             for g in range(ATT_GROUP)], axis=0)
        k_win = k_all[r0:r0 + 2 * WINDOW, kh * ATT_HEAD_DIM:(kh + 1) * ATT_HEAD_DIM]
        scores[kh, qi] = _dot_nt(k_win, qs) + (bias_first if qi == 0 else bias_rest)
    probs, denoms = {}, {}
    for kh, qi in blocks:
        s = scores[kh, qi]
        m = jnp.maximum(jnp.max(s, axis=0, keepdims=True), sinks[kh])
        p = jnp.exp(s - m)
        denoms[kh, qi] = jnp.sum(p, axis=0, keepdims=True) + jnp.exp(sinks[kh] - m)
        probs[kh, qi] = p.astype(BF16)
    outs = {}
    for kh, qi in blocks:
        r0 = qi * WINDOW
        vt_win = vt_all[kh * ATT_HEAD_DIM:(kh + 1) * ATT_HEAD_DIM, r0:r0 + 2 * WINDOW]
        outs[kh, qi] = (_dot(vt_win, probs[kh, qi]) / denoms[kh, qi]).astype(BF16)
    a_t = jnp.concatenate(
        [jnp.concatenate([outs[hd // ATT_GROUP, qi][:, (hd % ATT_GROUP) * WINDOW:(hd % ATT_GROUP + 1) * WINDOW]
                          for qi in range(tm // WINDOW)], axis=1)
         for hd in range(ATT_HEADS)], axis=0)

    u_ext = jnp.concatenate([u_hist, u], axis=0)
    first_rows = (lax.broadcasted_iota(jnp.int32, (POOL_HISTORY, POOL_GROUP), 0) + 1).astype(F32)
    pooled = []
    for gi, w in enumerate(POOL_WINDOWS):
        z = u_ext[:, gi * POOL_GROUP:(gi + 1) * POOL_GROUP]
        shift = 1
        while shift < w:
            z = z + pltpu.roll(z, shift, axis=0)
            shift *= 2
        head_scale = jnp.where(j == 0, 1.0 / jnp.minimum(first_rows, float(w)), 1.0 / w)
        mean = jnp.concatenate([z[POOL_HISTORY:2 * POOL_HISTORY, :] * head_scale,
                                z[2 * POOL_HISTORY:, :] * (1.0 / w)], axis=0)
        pooled.append((mean - u[:, gi * POOL_GROUP:(gi + 1) * POOL_GROUP]).astype(BF16))
    pool_out = jnp.concatenate(
        [_dot(pooled[gi], wpool_ref[gi]) for gi in range(len(POOL_WINDOWS))], axis=1).astype(BF16)

    o_ref[...] = (x + _dot_tn(a_t, w_out_ref[:ATT_WIDTH, :])
                  + _dot(pool_out, w_out_ref[ATT_WIDTH:, :]))
    kprev_ref[...] = kr[tm - WINDOW:, :]
    vtprev_ref[...] = vt_cur[:, tm - WINDOW:]
    uhist_ref[...] = u[tm - POOL_HISTORY:, :]


def _attention_bias():
    c = np.arange(2 * WINDOW)[:, None]
    r = np.arange(ATT_GROUP * WINDOW)[None, :] % WINDOW
    band = (c > r) & (c <= r + WINDOW)
    first = band & (c >= WINDOW)
    return jnp.asarray(np.where(np.stack([band, first]), 0.0, MASKED_SCORE), F32)


def _attn_pool_mixer(x3, gain, w_in, w_out, q_gain, k_gain, sinks, pool_w, pool_scale, cos_t, sin_t):
    b, t, d = x3.shape
    tm = min(AB_ROW_TILE, t)
    ab_in = w_in.shape[1]
    head_of_lane = np.arange(KV_WIDTH) // ATT_HEAD_DIM
    bd = jnp.asarray((head_of_lane[:, None] == head_of_lane[None, :]) / ATT_HEAD_DIM, BF16)
    n_groups = len(POOL_WINDOWS)
    wpool = (pool_w * pool_scale.reshape(n_groups, 1, POOL_GROUP)).astype(BF16)
    half = ATT_HEAD_DIM // 2

    def gain_rows(g, scale):
        partner = jnp.concatenate([g[half:], g[:half]])
        return jnp.stack([jnp.tile(g, ATT_KV_HEADS), jnp.tile(partner, ATT_KV_HEADS)]) * scale

    qg = gain_rows(q_gain, ATT_HEAD_DIM ** -0.5)
    kg = gain_rows(k_gain, 1.0)
    sink_lanes = jnp.repeat(sinks, WINDOW).reshape(ATT_KV_HEADS, ATT_GROUP * WINDOW)
    return pl.pallas_call(
        _ab_kernel,
        out_shape=jax.ShapeDtypeStruct((b, t, d), F32),
        grid=(b, t // tm),
        in_specs=[
            pl.BlockSpec((None, tm, d), lambda i, j: (i, j, 0)),
            _const_spec((d, ab_in)),
            pl.BlockSpec((tm, KV_WIDTH), lambda i, j: (j, 0)),
            pl.BlockSpec((tm, KV_WIDTH), lambda i, j: (j, 0)),
            _const_spec((2, KV_WIDTH)),
            _const_spec((2, KV_WIDTH)),
            _const_spec((KV_WIDTH, KV_WIDTH)),
            _const_spec((2, 2 * WINDOW, ATT_GROUP * WINDOW)),
            _const_spec((ATT_KV_HEADS, ATT_GROUP * WINDOW)),
            _const_spec((len(POOL_WINDOWS), POOL_GROUP, POOL_GROUP)),
            _const_spec((ATT_WIDTH + POOL_WIDTH, d)),
        ],
        out_specs=pl.BlockSpec((None, tm, d), lambda i, j: (i, j, 0)),
        scratch_shapes=[
            pltpu.VMEM((WINDOW, KV_WIDTH), BF16),
            pltpu.VMEM((KV_WIDTH, WINDOW), BF16),
            pltpu.VMEM((POOL_HISTORY, POOL_WIDTH), F32),
        ],
        compiler_params=pltpu.CompilerParams(
            dimension_semantics=("arbitrary", "arbitrary"), vmem_limit_bytes=VMEM_LIMIT_BYTES),
        name="attn_pool_mixer",
    )(x3, _fold_rows(gain, w_in), cos_t, sin_t, qg, kg, bd, _attention_bias(),
      sink_lanes, wpool, w_out.astype(BF16))


def _split_bf16(a):
    hi = a.astype(BF16)
    lo = (a - hi.astype(F32)).astype(BF16)
    return hi, lo


def _node_reference(g, node, row):
    n_rows, width = g.shape
    g3 = g.reshape(n_rows // node, node, width)
    return jnp.broadcast_to(g3[:, row:row + 1, :], g3.shape).reshape(n_rows, width)


def _node_halves(a, node):
    n_rows, width = a.shape
    a3 = a.reshape(n_rows // node, node, width)
    return a3[:, :node // 2, :], a3[:, node // 2:, :]


def _hgrn_node_sizes():
    sizes, n = [], 2 * HGRN_BASE_BLOCK
    while n <= HGRN_CHUNK:
        sizes.append(n)
        n *= 2
    return sizes


def _hgrn_level_codes():
    t = np.arange(HGRN_CHUNK)[:, None]
    s = np.arange(HGRN_CHUNK)[None, :]
    code = np.full((HGRN_CHUNK, HGRN_CHUNK), 1 << 20, np.int32)
    for k, n in reversed(list(enumerate(_hgrn_node_sizes(), start=1))):
        code[(t // n) == (s // n)] = k
    same_base = (t // HGRN_BASE_BLOCK) == (s // HGRN_BASE_BLOCK)
    code[same_base] = np.where(s <= t, 0, 1 << 20)[same_base]
    return jnp.asarray(code), jnp.asarray(s <= t, BF16)


def _hgrn_kernel(layer, x_ref, w_in_ref, lb_logits_ref, w_out_ref, level_ref, tri_ref, o_ref,
                 state_ref):
    tm = x_ref.shape[0]
    dk = HGRN_DK
    cl = HGRN_CHUNK
    j = pl.program_id(1)

    @pl.when(j == 0)
    def _():
        state_ref[...] = jnp.zeros_like(state_ref)

    x = x_ref[...]
    h = _rms_normalize(x).astype(BF16)

    logits = lb_logits_ref[...]
    e = jnp.exp(logits - jnp.max(logits, axis=0, keepdims=True))
    prob = e / jnp.sum(e, axis=0, keepdims=True)
    lb = jnp.sum(prob[:layer + 1, :], axis=0, keepdims=True) - prob[0:1, :]

    tri = tri_ref[...]
    tri2 = jnp.concatenate([tri, tri], axis=1)
    level = level_ref[...]
    base = HGRN_BASE_BLOCK
    diag_mask = level == 0
    node_sizes = _hgrn_node_sizes()
    node_masks = [level <= k + 1 for k in range(len(node_sizes))]

    chunks = range(tm // cl)

    def rows(a, c):
        return a[c * cl:(c + 1) * cl]

    def stage_project(hd, _):
        return _dot(h, w_in_ref[hd])

    def stage_gates(hd, proj):
        qh = proj[:, :dk]
        lb_h = lb[:, hd * dk:(hd + 1) * dk]
        f = lb_h + (1.0 - lb_h) * _sigmoid(proj[:, dk:2 * dk])
        return dict(qf=qh * _sigmoid(qh), key=1.0 - f,
                    lf=_split_bf16(jnp.log2(jnp.maximum(f, GATE_EPS))),
                    v=proj[:, 2 * dk:3 * dk].astype(BF16), gate=_sigmoid(proj[:, 3 * dk:]))

    def stage_cumsum(hd, c, w):
        w["g", c] = _dot(tri2, jnp.concatenate([rows(w["lf"][0], c), rows(w["lf"][1], c)], axis=0))

    def stage_operands(hd, c, w):
        g, qf, key = w["g", c], rows(w["qf"], c), rows(w["key"], c)
        ref0 = _node_reference(g, base, base // 2 - 1)
        ops = [((qf * jnp.exp2(g - ref0)).astype(BF16), (key * jnp.exp2(ref0 - g)).astype(BF16))]
        for n in node_sizes:
            g_l, g_r = _node_halves(g, n)
            ref = g_l[:, n // 2 - 1:n // 2, :]
            qt_r = (_node_halves(qf, n)[1] * jnp.exp2(g_r - ref)).astype(BF16)
            kt_l = (_node_halves(key, n)[0] * jnp.exp2(ref - g_l)).astype(BF16)
            zeros = jnp.zeros_like(qt_r)
            kt = jnp.concatenate([kt_l, zeros], axis=1).reshape(cl, dk)
            if n >= HGRN_COMPACT_NODE:
                ops.append((qt_r.reshape(cl // 2, dk), kt))
            else:
                ops.append((jnp.concatenate([zeros, qt_r], axis=1).reshape(cl, dk), kt))
        g_last = g[cl - 1:cl, :]
        w["levels", c] = ops
        w["q_head", c] = (qf * jnp.exp2(g)).astype(BF16)
        w["k_tail", c] = (key * jnp.exp2(g_last - g)).astype(BF16)
        w["decay", c] = jnp.exp2(g_last)

    def stage_scores(hd, c, w):
        ops = w["levels", c]
        a = jnp.where(diag_mask, _dot_nt(*ops[0]), 0.0)
        for n, nmask, (qt, kt) in zip(node_sizes, node_masks, ops[1:]):
            lvl = _dot_nt(qt, kt)
            if n >= HGRN_COMPACT_NODE:
                pieces = []
                for node in range(cl // n):
                    right = slice(node * n + n // 2, (node + 1) * n)
                    part = lvl[node * (n // 2):(node + 1) * (n // 2)]
                    pieces += [a[node * n:node * n + n // 2],
                               a[right] + (part if n == cl else jnp.where(nmask[right], part, 0.0))]
                a = jnp.concatenate(pieces, axis=0)
            else:
                a = a + jnp.where(nmask, lvl, 0.0)
        w["scores", c] = a.astype(BF16)

    def stage_recurrence(hd, c, w):
        st = states[hd]
        vv = rows(w["v"], c)
        w["out", c] = _dot(w["scores", c], vv) + _dot_nt(w["q_head", c], st.astype(BF16))
        states[hd] = st * w["decay", c] + _dot_tn(vv, w["k_tail", c])

    def gate_and_normalize(c, work):
        gated = [w["out", c] * rows(w["gate"], c) for w in work]
        for hd, w in enumerate(work):
            og = gated[hd]
            mean_sq = jnp.mean(og * og, axis=-1, keepdims=True)
            w["y", c] = (og * lax.rsqrt(mean_sq + NORM_EPS)).astype(BF16)

    heads = range(HGRN_HEADS)
    states = [state_ref[hd] for hd in heads]
    work = [stage_gates(hd, proj) for hd, proj in enumerate([stage_project(hd, None) for hd in heads])]
    for stage in (stage_cumsum, stage_operands, stage_scores, stage_recurrence):
        for c in chunks:
            for hd in heads:
                stage(hd, c, work[hd])
    for c in chunks:
        gate_and_normalize(c, work)

    for hd in heads:
        state_ref[hd] = states[hd]
    y = jnp.concatenate(
        [jnp.concatenate([work[hd]["y", c] for c in chunks], axis=0) for hd in heads], axis=1)
    o_ref[...] = x + _dot(y, w_out_ref[...])


def _hgrn_mixer(x3, gain, w_in, w_out, out_gain, lb_logits, layer):
    b, t, d = x3.shape
    tm = min(HGRN_ROW_TILE, t)
    n_layers, kw = lb_logits.shape
    w_heads = _fold_rows(gain, w_in).reshape(d, 4, HGRN_HEADS, HGRN_DK).transpose(2, 0, 1, 3).reshape(
        HGRN_HEADS, d, 4 * HGRN_DK)
    return pl.pallas_call(
        functools.partial(_hgrn_kernel, layer),
        out_shape=jax.ShapeDtypeStruct((b, t, d), F32),
        grid=(b, t // tm),
        in_specs=[
            pl.BlockSpec((None, tm, d), lambda i, j: (i, j, 0)),
            _const_spec((HGRN_HEADS, d, 4 * HGRN_DK)),
            _const_spec((n_layers, kw)),
            _const_spec((kw, d)),
            _const_spec((HGRN_CHUNK, HGRN_CHUNK)),
            _const_spec((HGRN_CHUNK, HGRN_CHUNK)),
        ],
        out_specs=pl.BlockSpec((None, tm, d), lambda i, j: (i, j, 0)),
        scratch_shapes=[
            pltpu.VMEM((HGRN_HEADS, HGRN_DK, HGRN_DK), F32),
        ],
        compiler_params=pltpu.CompilerParams(
            dimension_semantics=("arbitrary", "arbitrary"), vmem_limit_bytes=VMEM_LIMIT_BYTES),
        name="hgrn_mixer",
    )(x3, w_heads, lb_logits, _fold_rows(jnp.tile(out_gain, HGRN_HEADS), w_out), *_hgrn_level_codes())


def kernel(x, positions, norm_gains, ffn_w_gate, ffn_w_up, ffn_w_down, ab_w_in, ab_w_out, q_norm_gain, k_norm_gain, attn_sinks, pool_w, pool_scale, c_w_in, c_w_out, c_out_norm_gain, lb_logits):
    b, t, d = x.shape
    depth = norm_gains.shape[0]
    cos_t, sin_t = _rope_tables(positions)

    ffn_gain = jnp.stack([norm_gains[:, 0], norm_gains[:, 2]], axis=1)[..., None]
    wg_all = (ffn_gain * ffn_w_gate).astype(BF16)
    wu_all = (ffn_gain * ffn_w_up).astype(BF16)
    wd_all = (0.5 * ffn_w_down).astype(BF16)

    def ffn(x3, layer, which):
        return _ffn(x3.reshape(b * t, d), wg_all, wu_all, wd_all, layer, which).reshape(b, t, d)

    for layer in range(depth):
        x = ffn(x, layer, 0)
        jx = layer // 2
        if layer % 2 == 0:
            x = _attn_pool_mixer(x, norm_gains[layer, 1], ab_w_in[jx], ab_w_out[jx], q_norm_gain[jx],
                                 k_norm_gain[jx], attn_sinks[jx], pool_w[jx], pool_scale[jx], cos_t, sin_t)
        else:
            x = _hgrn_mixer(x, norm_gains[layer, 1], c_w_in[jx], c_w_out[jx], c_out_norm_gain[jx],
                            lb_logits, jx)
        x = ffn(x, layer, 1)
    return x
```

```python
import functools

import jax
import jax.numpy as jnp
import numpy as np
from jax import lax
from jax.experimental import pallas as pl
from jax.experimental.pallas import tpu as pltpu

F32 = jnp.float32
BF16 = jnp.bfloat16

NORM_EPS = 1e-6
GATE_EPS = 1e-6
ROPE_THETA = 10000.0

ATT_HEADS = 8
ATT_KV_HEADS = 2
ATT_GROUP = ATT_HEADS // ATT_KV_HEADS
ATT_HEAD_DIM = 64
WINDOW = 128
ATT_WIDTH = ATT_HEADS * ATT_HEAD_DIM
KV_WIDTH = ATT_KV_HEADS * ATT_HEAD_DIM
POOL_WINDOWS = (2, 4, 8, 16)
POOL_GROUP = 128
POOL_WIDTH = POOL_GROUP * len(POOL_WINDOWS)
POOL_HISTORY = 16
HGRN_HEADS = 8
HGRN_DK = 128

FFN_ROW_TILE = 1024
FFN_COL_CHUNK = 256
AB_ROW_TILE = 1024
HGRN_ROW_TILE = 1024
HGRN_CHUNK = 128
HGRN_BASE_BLOCK = 8
HGRN_COMPACT_NODE = 16
VMEM_LIMIT_BYTES = 56 * 1024 * 1024
MASKED_SCORE = -1e30


def _rms_normalize(x):
    ms = jnp.mean(x * x, axis=-1, keepdims=True)
    return x * lax.rsqrt(ms + NORM_EPS)


def _fold_rows(gain, w):
    return (gain[:, None] * w).astype(BF16)


def _sigmoid(x):
    return 0.5 * jnp.tanh(0.5 * x) + 0.5


def _const_spec(shape):
    nd = len(shape)
    return pl.BlockSpec(shape, lambda *_: (0,) * nd, pipeline_mode=pl.Buffered(1))


def _dot(a, b):
    return jnp.dot(a, b, preferred_element_type=F32)


def _dot_nt(a, b):
    return lax.dot_general(a, b, (((1,), (1,)), ((), ())), preferred_element_type=F32)


def _dot_tn(a, b):
    return lax.dot_general(a, b, (((0,), (0,)), ((), ())), preferred_element_type=F32)


def _ffn_kernel(x_ref, wg_ref, wu_ref, wd_ref, o_ref):
    x = x_ref[...]
    h = _rms_normalize(x).astype(BF16)
    d_ff = wg_ref.shape[1]
    acc = x
    for c in range(d_ff // FFN_COL_CHUNK):
        sl = slice(c * FFN_COL_CHUNK, (c + 1) * FFN_COL_CHUNK)
        g = _dot(h, wg_ref[:, sl])
        u = _dot(h, wu_ref[:, sl])
        a = (g * jax.nn.sigmoid(g) * u).astype(BF16)
        acc = acc + _dot(a, wd_ref[sl, :])
    o_ref[...] = acc


def _ffn(x2, wg, wu, wd, layer, which):
    n, d = x2.shape
    d_ff = wg.shape[-1]
    tm = min(FFN_ROW_TILE, n)

    def weight_spec(rows, cols):
        return pl.BlockSpec((None, None, rows, cols), lambda i: (layer, which, 0, 0),
                            pipeline_mode=pl.Buffered(1))

    return pl.pallas_call(
        _ffn_kernel,
        out_shape=jax.ShapeDtypeStruct((n, d), F32),
        grid=(n // tm,),
        in_specs=[
            pl.BlockSpec((tm, d), lambda i: (i, 0)),
            weight_spec(d, d_ff),
            weight_spec(d, d_ff),
            weight_spec(d_ff, d),
        ],
        out_specs=pl.BlockSpec((tm, d), lambda i: (i, 0)),
        compiler_params=pltpu.CompilerParams(
            dimension_semantics=("arbitrary",), vmem_limit_bytes=VMEM_LIMIT_BYTES),
        name="ffn",
    )(x2, wg, wu, wd)


def _rope_table_kernel(pos_ref, inv_freq_ref, sign_ref, cos_ref, sin_ref):
    ang = pos_ref[...].astype(F32) * inv_freq_ref[...]
    cos_ref[...] = jnp.cos(ang)
    sin_ref[...] = jnp.sin(ang) * sign_ref[...]


def _rope_tables(positions):
    t = positions.shape[0]
    half = ATT_HEAD_DIM // 2
    lane = np.arange(KV_WIDTH)
    inv_freq = ROPE_THETA ** (-jnp.arange(half, dtype=F32) / half)
    inv_freq_lanes = jnp.tile(inv_freq, KV_WIDTH // half).reshape(1, KV_WIDTH)
    sign = jnp.asarray(np.where(lane % ATT_HEAD_DIM < half, -1.0, 1.0), F32).reshape(1, KV_WIDTH)
    tr = min(256, t)
    return pl.pallas_call(
        _rope_table_kernel,
        out_shape=(jax.ShapeDtypeStruct((t, KV_WIDTH), F32),) * 2,
        grid=(t // tr,),
        in_specs=[pl.BlockSpec((tr, 1), lambda i: (i, 0)),
                  _const_spec((1, KV_WIDTH)), _const_spec((1, KV_WIDTH))],
        out_specs=(pl.BlockSpec((tr, KV_WIDTH), lambda i: (i, 0)),) * 2,
        compiler_params=pltpu.CompilerParams(dimension_semantics=("arbitrary",)),
        name="rope_tables",
    )(positions.reshape(t, 1), inv_freq_lanes, sign)


def _rotate_half_partner(z):
    w = z.shape[1]
    lane = lax.broadcasted_iota(jnp.int32, z.shape, 1)
    first_half = (lane & (ATT_HEAD_DIM // 2)) == 0
    return jnp.where(first_half, pltpu.roll(z, w - ATT_HEAD_DIM // 2, axis=1),
                     pltpu.roll(z, ATT_HEAD_DIM // 2, axis=1))


def _ab_kernel(x_ref, w_in_ref, cos_ref, sin_ref, qg_ref, kg_ref, bd_ref, bias_ref,
               sink_ref, wpool_ref, w_out_ref, o_ref,
               kprev_ref, vtprev_ref, uhist_ref):
    tm = x_ref.shape[0]
    j = pl.program_id(1)

    @pl.when(j == 0)
    def _():
        kprev_ref[...] = jnp.zeros_like(kprev_ref)
        vtprev_ref[...] = jnp.zeros_like(vtprev_ref)
        uhist_ref[...] = jnp.zeros_like(uhist_ref)

    k_prev = kprev_ref[...]
    vt_prev = vtprev_ref[...]
    u_hist = uhist_ref[...]
    bias_rest = bias_ref[0]
    bias_first = bias_ref[jnp.where(j == 0, 1, 0)]

    x = x_ref[...]
    h = _rms_normalize(x).astype(BF16)
    proj = _dot(h, w_in_ref[...])
    q = proj[:, :ATT_WIDTH]
    k = proj[:, ATT_WIDTH:ATT_WIDTH + KV_WIDTH]
    v = proj[:, ATT_WIDTH + KV_WIDTH:ATT_WIDTH + 2 * KV_WIDTH]
    u = proj[:, ATT_WIDTH + 2 * KV_WIDTH:]

    bd = bd_ref[...]
    q_sq = (q * q).astype(BF16)
    q_ms = jnp.concatenate(
        [_dot(q_sq[:, g * KV_WIDTH:(g + 1) * KV_WIDTH], bd) for g in range(ATT_WIDTH // KV_WIDTH)], axis=1)
    k_ms = _dot((k * k).astype(BF16), bd)
    cos_kv = cos_ref[...]
    sin_kv = sin_ref[...]
    groups = ATT_WIDTH // KV_WIDTH
    q_cos = jnp.concatenate([cos_kv * qg_ref[0:1, :]] * groups, axis=1)
    q_sin = jnp.concatenate([sin_kv * qg_ref[1:2, :]] * groups, axis=1)
    qr = ((q * q_cos + _rotate_half_partner(q) * q_sin) * lax.rsqrt(q_ms + NORM_EPS)).astype(BF16)
    kr = ((k * (cos_kv * kg_ref[0:1, :]) + _rotate_half_partner(k) * (sin_kv * kg_ref[1:2, :]))
          * lax.rsqrt(k_ms + NORM_EPS)).astype(BF16)
    k_all = jnp.concatenate([k_prev, kr], axis=0)
    vt_cur = v.T.astype(BF16)
    vt_all = jnp.concatenate([vt_prev, vt_cur], axis=1)

    blocks = [(kh, qi) for kh in range(ATT_KV_HEADS) for qi in range(tm // WINDOW)]
    sinks = [sink_ref[kh:kh + 1, :] for kh in range(ATT_KV_HEADS)]
    scores = {}
    for kh, qi in blocks:
        r0 = qi * WINDOW
        qs = jnp.concatenate(
            [qr[r0:r0 + WINDOW, (kh * ATT_GROUP + g) * ATT_HEAD_DIM:(kh * ATT_GROUP + g + 1) * ATT_HEAD_DIM]
             for g in range(ATT_GROUP)], axis=0)
        k_win = k_all[r0:r0 + 2 * WINDOW, kh * ATT_HEAD_DIM:(kh + 1) * ATT_HEAD_DIM]
        scores[kh, qi] = _dot_nt(k_win, qs) + (bias_first if qi == 0 else bias_rest)
    probs, denoms = {}, {}
    for kh, qi in blocks:
        s = scores[kh, qi]
        m = jnp.maximum(jnp.max(s, axis=0, keepdims=True), sinks[kh])
        p = jnp.exp(s - m)
        denoms[kh, qi] = jnp.sum(p, axis=0, keepdims=True) + jnp.exp(sinks[kh] - m)
        probs[kh, qi] = p.astype(BF16)
    outs = {}
    for kh, qi in blocks:
        r0 = qi * WINDOW
        vt_win = vt_all[kh * ATT_HEAD_DIM:(kh + 1) * ATT_HEAD_DIM, r0:r0 + 2 * WINDOW]
        outs[kh, qi] = (_dot(vt_win, probs[kh, qi]) / denoms[kh, qi]).astype(BF16)
    a_t = jnp.concatenate(
        [jnp.concatenate([outs[hd // ATT_GROUP, qi][:, (hd % ATT_GROUP) * WINDOW:(hd % ATT_GROUP + 1) * WINDOW]
                          for qi in range(tm // WINDOW)], axis=1)
         for hd in range(ATT_HEADS)], axis=0)

    u_ext = jnp.concatenate([u_hist, u], axis=0)
    first_rows = (lax.broadcasted_iota(jnp.int32, (POOL_HISTORY, POOL_GROUP), 0) + 1).astype(F32)
    pooled = []
    for gi, w in enumerate(POOL_WINDOWS):
        z = u_ext[:, gi * POOL_GROUP:(gi + 1) * POOL_GROUP]
        shift = 1
        while shift < w:
            z = z + pltpu.roll(z, shift, axis=0)
            shift *= 2
        head_scale = jnp.where(j == 0, 1.0 / jnp.minimum(first_rows, float(w)), 1.0 / w)
        mean = jnp.concatenate([z[POOL_HISTORY:2 * POOL_HISTORY, :] * head_scale,
                                z[2 * POOL_HISTORY:, :] * (1.0 / w)], axis=0)
        pooled.append((mean - u[:, gi * POOL_GROUP:(gi + 1) * POOL_GROUP]).astype(BF16))
    pool_out = jnp.concatenate(
        [_dot(pooled[gi], wpool_ref[gi]) for gi in range(len(POOL_WINDOWS))], axis=1).astype(BF16)

    o_ref[...] = (x + _dot_tn(a_t, w_out_ref[:ATT_WIDTH, :])
                  + _dot(pool_out, w_out_ref[ATT_WIDTH:, :]))
    kprev_ref[...] = kr[tm - WINDOW:, :]
    vtprev_ref[...] = vt_cur[:, tm - WINDOW:]
    uhist_ref[...] = u[tm - POOL_HISTORY:, :]


def _attention_bias():
    c = np.arange(2 * WINDOW)[:, None]
    r = np.arange(ATT_GROUP * WINDOW)[None, :] % WINDOW
    band = (c > r) & (c <= r + WINDOW)
    first = band & (c >= WINDOW)
    return jnp.asarray(np.where(np.stack([band, first]), 0.0, MASKED_SCORE), F32)


def _attn_pool_mixer(x3, gain, w_in, w_out, q_gain, k_gain, sinks, pool_w, pool_scale, cos_t, sin_t):
    b, t, d = x3.shape
    tm = min(AB_ROW_TILE, t)
    ab_in = w_in.shape[1]
    head_of_lane = np.arange(KV_WIDTH) // ATT_HEAD_DIM
    bd = jnp.asarray((head_of_lane[:, None] == head_of_lane[None, :]) / ATT_HEAD_DIM, BF16)
    n_groups = len(POOL_WINDOWS)
    wpool = (pool_w * pool_scale.reshape(n_groups, 1, POOL_GROUP)).astype(BF16)
    half = ATT_HEAD_DIM // 2

    def gain_rows(g, scale):
        partner = jnp.concatenate([g[half:], g[:half]])
        return jnp.stack([jnp.tile(g, ATT_KV_HEADS), jnp.tile(partner, ATT_KV_HEADS)]) * scale

    qg = gain_rows(q_gain, ATT_HEAD_DIM ** -0.5)
    kg = gain_rows(k_gain, 1.0)
    sink_lanes = jnp.repeat(sinks, WINDOW).reshape(ATT_KV_HEADS, ATT_GROUP * WINDOW)
    return pl.pallas_call(
        _ab_kernel,
        out_shape=jax.ShapeDtypeStruct((b, t, d), F32),
        grid=(b, t // tm),
        in_specs=[
            pl.BlockSpec((None, tm, d), lambda i, j: (i, j, 0)),
            _const_spec((d, ab_in)),
            pl.BlockSpec((tm, KV_WIDTH), lambda i, j: (j, 0)),
            pl.BlockSpec((tm, KV_WIDTH), lambda i, j: (j, 0)),
            _const_spec((2, KV_WIDTH)),
            _const_spec((2, KV_WIDTH)),
            _const_spec((KV_WIDTH, KV_WIDTH)),
            _const_spec((2, 2 * WINDOW, ATT_GROUP * WINDOW)),
            _const_spec((ATT_KV_HEADS, ATT_GROUP * WINDOW)),
            _const_spec((len(POOL_WINDOWS), POOL_GROUP, POOL_GROUP)),
            _const_spec((ATT_WIDTH + POOL_WIDTH, d)),
        ],
        out_specs=pl.BlockSpec((None, tm, d), lambda i, j: (i, j, 0)),
        scratch_shapes=[
            pltpu.VMEM((WINDOW, KV_WIDTH), BF16),
            pltpu.VMEM((KV_WIDTH, WINDOW), BF16),
            pltpu.VMEM((POOL_HISTORY, POOL_WIDTH), F32),
        ],
        compiler_params=pltpu.CompilerParams(
            dimension_semantics=("arbitrary", "arbitrary"), vmem_limit_bytes=VMEM_LIMIT_BYTES),
        name="attn_pool_mixer",
    )(x3, _fold_rows(gain, w_in), cos_t, sin_t, qg, kg, bd, _attention_bias(),
      sink_lanes, wpool, w_out.astype(BF16))


def _split_bf16(a):
    hi = a.astype(BF16)
    lo = (a - hi.astype(F32)).astype(BF16)
    return hi, lo


def _node_reference(g, node, row):
    n_rows, width = g.shape
    g3 = g.reshape(n_rows // node, node, width)
    return jnp.broadcast_to(g3[:, row:row + 1, :], g3.shape).reshape(n_rows, width)


def _node_halves(a, node):
    n_rows, width = a.shape
    a3 = a.reshape(n_rows // node, node, width)
    return a3[:, :node // 2, :], a3[:, node // 2:, :]


def _hgrn_node_sizes():
    sizes, n = [], 2 * HGRN_BASE_BLOCK
    while n <= HGRN_CHUNK:
        sizes.append(n)
        n *= 2
    return sizes


def _hgrn_level_codes():
    t = np.arange(HGRN_CHUNK)[:, None]
    s = np.arange(HGRN_CHUNK)[None, :]
    code = np.full((HGRN_CHUNK, HGRN_CHUNK), 1 << 20, np.int32)
    for k, n in reversed(list(enumerate(_hgrn_node_sizes(), start=1))):
        code[(t // n) == (s // n)] = k
    same_base = (t // HGRN_BASE_BLOCK) == (s // HGRN_BASE_BLOCK)
    code[same_base] = np.where(s <= t, 0, 1 << 20)[same_base]
    return jnp.asarray(code), jnp.asarray(s <= t, BF16)


def _hgrn_kernel(layer, x_ref, w_in_ref, lb_logits_ref, w_out_ref, level_ref, tri_ref, o_ref,
                 state_ref):
    tm = x_ref.shape[0]
    dk = HGRN_DK
    cl = HGRN_CHUNK
    j = pl.program_id(1)

    @pl.when(j == 0)
    def _():
        state_ref[...] = jnp.zeros_like(state_ref)

    x = x_ref[...]
    h = _rms_normalize(x).astype(BF16)

    logits = lb_logits_ref[...]
    e = jnp.exp(logits - jnp.max(logits, axis=0, keepdims=True))
    prob = e / jnp.sum(e, axis=0, keepdims=True)
    lb = jnp.sum(prob[:layer + 1, :], axis=0, keepdims=True) - prob[0:1, :]

    tri = tri_ref[...]
    tri2 = jnp.concatenate([tri, tri], axis=1)
    level = level_ref[...]
    base = HGRN_BASE_BLOCK
    diag_mask = level == 0
    node_sizes = _hgrn_node_sizes()
    node_masks = [level <= k + 1 for k in range(len(node_sizes))]

    chunks = range(tm // cl)

    def rows(a, c):
        return a[c * cl:(c + 1) * cl]

    def stage_project(hd, _):
        return _dot(h, w_in_ref[hd])

    def stage_gates(hd, proj):
        qh = proj[:, :dk]
        lb_h = lb[:, hd * dk:(hd + 1) * dk]
        f = lb_h + (1.0 - lb_h) * _sigmoid(proj[:, dk:2 * dk])
        return dict(qf=qh * _sigmoid(qh), key=1.0 - f,
                    lf=_split_bf16(jnp.log2(jnp.maximum(f, GATE_EPS))),
                    v=proj[:, 2 * dk:3 * dk].astype(BF16), gate=_sigmoid(proj[:, 3 * dk:]))

    def cumsum_pair(c, w_a, w_b):
        hi = jnp.concatenate([rows(w_a["lf"][0], c), rows(w_b["lf"][0], c)], axis=1)
        lo = jnp.concatenate([rows(w_a["lf"][1], c), rows(w_b["lf"][1], c)], axis=1)
        both = _dot(tri2, jnp.concatenate([hi, lo], axis=0))
        w_a["g", c] = both[:, :dk]
        w_b["g", c] = both[:, dk:]

    def stage_operands(hd, c, w):
        g, qf, key = w["g", c], rows(w["qf"], c), rows(w["key"], c)
        ref0 = _node_reference(g, base, base // 2 - 1)
        ops = [((qf * jnp.exp2(g - ref0)).astype(BF16), (key * jnp.exp2(ref0 - g)).astype(BF16))]
        for n in node_sizes:
            g_l, g_r = _node_halves(g, n)
            ref = g_l[:, n // 2 - 1:n // 2, :]
            qt_r = (_node_halves(qf, n)[1] * jnp.exp2(g_r - ref)).astype(BF16)
            kt_l = (_node_halves(key, n)[0] * jnp.exp2(ref - g_l)).astype(BF16)
            zeros = jnp.zeros_like(qt_r)
            kt = jnp.concatenate([kt_l, zeros], axis=1).reshape(cl, dk)
            if n >= HGRN_COMPACT_NODE:
                ops.append((qt_r.reshape(cl // 2, dk), kt))
            else:
                ops.append((jnp.concatenate([zeros, qt_r], axis=1).reshape(cl, dk), kt))
        g_last = g[cl - 1:cl, :]
        w["levels", c] = ops
        w["q_head", c] = (qf * jnp.exp2(g)).astype(BF16)
        w["k_tail", c] = (key * jnp.exp2(g_last - g)).astype(BF16)
        w["decay", c] = jnp.exp2(g_last)

    def stage_scores(hd, c, w):
        ops = w["levels", c]
        a = jnp.where(diag_mask, _dot_nt(*ops[0]), 0.0)
        for n, nmask, (qt, kt) in zip(node_sizes, node_masks, ops[1:]):
            lvl = _dot_nt(qt, kt)
            if n >= HGRN_COMPACT_NODE:
                pieces = []
                for node in range(cl // n):
                    right = slice(node * n + n // 2, (node + 1) * n)
                    part = lvl[node * (n // 2):(node + 1) * (n // 2)]
                    pieces += [a[node * n:node * n + n // 2],
                               a[right] + (part if n == cl else jnp.where(nmask[right], part, 0.0))]
                a = jnp.concatenate(pieces, axis=0)
            else:
                a = a + jnp.where(nmask, lvl, 0.0)
        w["scores", c] = a.astype(BF16)

    def stage_recurrence(hd, c, w):
        st = states[hd]
        vv = rows(w["v"], c)
        w["out", c] = _dot(w["scores", c], vv) + _dot_nt(w["q_head", c], st.astype(BF16))
        states[hd] = st * w["decay", c] + _dot_tn(vv, w["k_tail", c])

    def gate_and_normalize(c, work):
        gated = [w["out", c] * rows(w["gate"], c) for w in work]
        for hd, w in enumerate(work):
            og = gated[hd]
            mean_sq = jnp.mean(og * og, axis=-1, keepdims=True)
            w["y", c] = (og * lax.rsqrt(mean_sq + NORM_EPS)).astype(BF16)

    heads = range(HGRN_HEADS)
    states = [state_ref[hd] for hd in heads]
    work = [stage_gates(hd, proj) for hd, proj in enumerate([stage_project(hd, None) for hd in heads])]
    for c in chunks:
        for hd in range(0, HGRN_HEADS, 2):
            cumsum_pair(c, work[hd], work[hd + 1])
    for stage in (stage_operands, stage_scores, stage_recurrence):
        for c in chunks:
            for hd in heads:
                stage(hd, c, work[hd])
    for c in chunks:
        gate_and_normalize(c, work)

    for hd in heads:
        state_ref[hd] = states[hd]
    y = jnp.concatenate(
        [jnp.concatenate([work[hd]["y", c] for c in chunks], axis=0) for hd in heads], axis=1)
    o_ref[...] = x + _dot(y, w_out_ref[...])


def _hgrn_mixer(x3, gain, w_in, w_out, out_gain, lb_logits, layer):
    b, t, d = x3.shape
    tm = min(HGRN_ROW_TILE, t)
    n_layers, kw = lb_logits.shape
    w_heads = _fold_rows(gain, w_in).reshape(d, 4, HGRN_HEADS, HGRN_DK).transpose(2, 0, 1, 3).reshape(
        HGRN_HEADS, d, 4 * HGRN_DK)
    return pl.pallas_call(
        functools.partial(_hgrn_kernel, layer),
        out_shape=jax.ShapeDtypeStruct((b, t, d), F32),
        grid=(b, t // tm),
        in_specs=[
            pl.BlockSpec((None, tm, d), lambda i, j: (i, j, 0)),
            _const_spec((HGRN_HEADS, d, 4 * HGRN_DK)),
            _const_spec((n_layers, kw)),
            _const_spec((kw, d)),
            _const_spec((HGRN_CHUNK, HGRN_CHUNK)),
            _const_spec((HGRN_CHUNK, HGRN_CHUNK)),
        ],
        out_specs=pl.BlockSpec((None, tm, d), lambda i, j: (i, j, 0)),
        scratch_shapes=[
            pltpu.VMEM((HGRN_HEADS, HGRN_DK, HGRN_DK), F32),
        ],
        compiler_params=pltpu.CompilerParams(
            dimension_semantics=("arbitrary", "arbitrary"), vmem_limit_bytes=VMEM_LIMIT_BYTES),
        name="hgrn_mixer",
    )(x3, w_heads, lb_logits, _fold_rows(jnp.tile(out_gain, HGRN_HEADS), w_out), *_hgrn_level_codes())


def kernel(x, positions, norm_gains, ffn_w_gate, ffn_w_up, ffn_w_down, ab_w_in, ab_w_out, q_norm_gain, k_norm_gain, attn_sinks, pool_w, pool_scale, c_w_in, c_w_out, c_out_norm_gain, lb_logits):
    b, t, d = x.shape
    depth = norm_gains.shape[0]
    cos_t, sin_t = _rope_tables(positions)

    ffn_gain = jnp.stack([norm_gains[:, 0], norm_gains[:, 2]], axis=1)[..., None]
    wg_all = (ffn_gain * ffn_w_gate).astype(BF16)
    wu_all = (ffn_gain * ffn_w_up).astype(BF16)
    wd_all = (0.5 * ffn_w_down).astype(BF16)

    def ffn(x3, layer, which):
        return _ffn(x3.reshape(b * t, d), wg_all, wu_all, wd_all, layer, which).reshape(b, t, d)

    for layer in range(depth):
        x = ffn(x, layer, 0)
        jx = layer // 2
        if layer % 2 == 0:
            x = _attn_pool_mixer(x, norm_gains[layer, 1], ab_w_in[jx], ab_w_out[jx], q_norm_gain[jx],
                                 k_norm_gain[jx], attn_sinks[jx], pool_w[jx], pool_scale[jx], cos_t, sin_t)
        else:
            x = _hgrn_mixer(x, norm_gains[layer, 1], c_w_in[jx], c_w_out[jx], c_out_norm_gain[jx],
                            lb_logits, jx)
        x = ffn(x, layer, 1)
    return x
```

```python
import functools

import jax
import jax.numpy as jnp
import numpy as np
from jax import lax
from jax.experimental import pallas as pl
from jax.experimental.pallas import tpu as pltpu

F32 = jnp.float32
BF16 = jnp.bfloat16

NORM_EPS = 1e-6
GATE_EPS = 1e-6
ROPE_THETA = 10000.0

ATT_HEADS = 8
ATT_KV_HEADS = 2
ATT_GROUP = ATT_HEADS // ATT_KV_HEADS
ATT_HEAD_DIM = 64
WINDOW = 128
ATT_WIDTH = ATT_HEADS * ATT_HEAD_DIM
KV_WIDTH = ATT_KV_HEADS * ATT_HEAD_DIM
POOL_WINDOWS = (2, 4, 8, 16)
POOL_GROUP = 128
POOL_WIDTH = POOL_GROUP * len(POOL_WINDOWS)
POOL_HISTORY = 16
HGRN_HEADS = 8
HGRN_DK = 128

FFN_ROW_TILE = 1024
FFN_COL_CHUNK = 256
AB_ROW_TILE = 1024
HGRN_ROW_TILE = 1024
HGRN_CHUNK = 128
HGRN_BASE_BLOCK = 8
HGRN_COMPACT_NODE = 16
VMEM_LIMIT_BYTES = 56 * 1024 * 1024
MASKED_SCORE = -1e30


def _rms_normalize(x):
    ms = jnp.mean(x * x, axis=-1, keepdims=True)
    return x * lax.rsqrt(ms + NORM_EPS)


def _fold_rows(gain, w):
    return (gain[:, None] * w).astype(BF16)


def _sigmoid(x):
    return 0.5 * jnp.tanh(0.5 * x) + 0.5


def _const_spec(shape):
    nd = len(shape)
    return pl.BlockSpec(shape, lambda *_: (0,) * nd, pipeline_mode=pl.Buffered(1))


def _dot(a, b):
    return jnp.dot(a, b, preferred_element_type=F32)


def _dot_nt(a, b):
    return lax.dot_general(a, b, (((1,), (1,)), ((), ())), preferred_element_type=F32)


def _dot_tn(a, b):
    return lax.dot_general(a, b, (((0,), (0,)), ((), ())), preferred_element_type=F32)


def _ffn_kernel(x_ref, wg_ref, wu_ref, wd_ref, o_ref):
    x = x_ref[...]
    h = _rms_normalize(x).astype(BF16)
    d_ff = wg_ref.shape[1]
    acc = x
    for c in range(d_ff // FFN_COL_CHUNK):
        sl = slice(c * FFN_COL_CHUNK, (c + 1) * FFN_COL_CHUNK)
        g = _dot(h, wg_ref[:, sl])
        u = _dot(h, wu_ref[:, sl])
        a = (g * jax.nn.sigmoid(g) * u).astype(BF16)
        acc = acc + _dot(a, wd_ref[sl, :])
    o_ref[...] = acc


def _ffn(x2, wg, wu, wd, layer, which):
    n, d = x2.shape
    d_ff = wg.shape[-1]
    tm = min(FFN_ROW_TILE, n)

    def weight_spec(rows, cols):
        return pl.BlockSpec((None, None, rows, cols), lambda i: (layer, which, 0, 0),
                            pipeline_mode=pl.Buffered(1))

    return pl.pallas_call(
        _ffn_kernel,
        out_shape=jax.ShapeDtypeStruct((n, d), F32),
        grid=(n // tm,),
        in_specs=[
            pl.BlockSpec((tm, d), lambda i: (i, 0)),
            weight_spec(d, d_ff),
            weight_spec(d, d_ff),
            weight_spec(d_ff, d),
        ],
        out_specs=pl.BlockSpec((tm, d), lambda i: (i, 0)),
        compiler_params=pltpu.CompilerParams(
            dimension_semantics=("arbitrary",), vmem_limit_bytes=VMEM_LIMIT_BYTES),
        name="ffn",
    )(x2, wg, wu, wd)


def _rope_table_kernel(pos_ref, inv_freq_ref, sign_ref, cos_ref, sin_ref):
    ang = pos_ref[...].astype(F32) * inv_freq_ref[...]
    cos_ref[...] = jnp.cos(ang)
    sin_ref[...] = jnp.sin(ang) * sign_ref[...]


def _rope_tables(positions):
    t = positions.shape[0]
    half = ATT_HEAD_DIM // 2
    lane = np.arange(KV_WIDTH)
    inv_freq = ROPE_THETA ** (-jnp.arange(half, dtype=F32) / half)
    inv_freq_lanes = jnp.tile(inv_freq, KV_WIDTH // half).reshape(1, KV_WIDTH)
    sign = jnp.asarray(np.where(lane % ATT_HEAD_DIM < half, -1.0, 1.0), F32).reshape(1, KV_WIDTH)
    tr = min(256, t)
    return pl.pallas_call(
        _rope_table_kernel,
        out_shape=(jax.ShapeDtypeStruct((t, KV_WIDTH), F32),) * 2,
        grid=(t // tr,),
        in_specs=[pl.BlockSpec((tr, 1), lambda i: (i, 0)),
                  _const_spec((1, KV_WIDTH)), _const_spec((1, KV_WIDTH))],
        out_specs=(pl.BlockSpec((tr, KV_WIDTH), lambda i: (i, 0)),) * 2,
        compiler_params=pltpu.CompilerParams(dimension_semantics=("arbitrary",)),
        name="rope_tables",
    )(positions.reshape(t, 1), inv_freq_lanes, sign)


def _rotate_half_partner(z):
    w = z.shape[1]
    lane = lax.broadcasted_iota(jnp.int32, z.shape, 1)
    first_half = (lane & (ATT_HEAD_DIM // 2)) == 0
    return jnp.where(first_half, pltpu.roll(z, w - ATT_HEAD_DIM // 2, axis=1),
                     pltpu.roll(z, ATT_HEAD_DIM // 2, axis=1))


def _ab_kernel(x_ref, w_in_ref, cos_ref, sin_ref, qg_ref, kg_ref, bd_ref, bias_ref,
               sink_ref, wpool_ref, w_out_ref, o_ref,
               kprev_ref, vtprev_ref, uhist_ref):
    tm = x_ref.shape[0]
    j = pl.program_id(1)

    @pl.when(j == 0)
    def _():
        kprev_ref[...] = jnp.zeros_like(kprev_ref)
        vtprev_ref[...] = jnp.zeros_like(vtprev_ref)
        uhist_ref[...] = jnp.zeros_like(uhist_ref)

    k_prev = kprev_ref[...]
    vt_prev = vtprev_ref[...]
    u_hist = uhist_ref[...]
    bias_rest = bias_ref[0]
    bias_first = bias_ref[jnp.where(j == 0, 1, 0)]

    x = x_ref[...]
    h = _rms_normalize(x).astype(BF16)
    proj = _dot(h, w_in_ref[...])
    q = proj[:, :ATT_WIDTH]
    k = proj[:, ATT_WIDTH:ATT_WIDTH + KV_WIDTH]
    v = proj[:, ATT_WIDTH + KV_WIDTH:ATT_WIDTH + 2 * KV_WIDTH]
    u = proj[:, ATT_WIDTH + 2 * KV_WIDTH:]

    bd = bd_ref[...]
    q_sq = (q * q).astype(BF16)
    q_ms = jnp.concatenate(
        [_dot(q_sq[:, g * KV_WIDTH:(g + 1) * KV_WIDTH], bd) for g in range(ATT_WIDTH // KV_WIDTH)], axis=1)
    k_ms = _dot((k * k).astype(BF16), bd)
    cos_kv = cos_ref[...]
    sin_kv = sin_ref[...]
    groups = ATT_WIDTH // KV_WIDTH
    q_cos = jnp.concatenate([cos_kv * qg_ref[0:1, :]] * groups, axis=1)
    q_sin = jnp.concatenate([sin_kv * qg_ref[1:2, :]] * groups, axis=1)
    qr = ((q * q_cos + _rotate_half_partner(q) * q_sin) * lax.rsqrt(q_ms + NORM_EPS)).astype(BF16)
    kr = ((k * (cos_kv * kg_ref[0:1, :]) + _rotate_half_partner(k) * (sin_kv * kg_ref[1:2, :]))
          * lax.rsqrt(k_ms + NORM_EPS)).astype(BF16)
    k_all = jnp.concatenate([k_prev, kr], axis=0)
    vt_cur = v.T.astype(BF16)
    vt_all = jnp.concatenate([vt_prev, vt_cur], axis=1)

    blocks = [(kh, qi) for kh in range(ATT_KV_HEADS) for qi in range(tm // WINDOW)]
    sinks = [sink_ref[kh:kh + 1, :] for kh in range(ATT_KV_HEADS)]
    scores = {}
    for kh, qi in blocks:
        r0 = qi * WINDOW
        qs = jnp.concatenate(
            [qr[r0:r0 + WINDOW, (kh * ATT_GROUP + g) * ATT_HEAD_DIM:(kh * ATT_GROUP + g + 1) * ATT_HEAD_DIM]
             for g in range(ATT_GROUP)], axis=0)
        k_win = k_all[r0:r0 + 2 * WINDOW, kh * ATT_HEAD_DIM:(kh + 1) * ATT_HEAD_DIM]
        scores[kh, qi] = _dot_nt(k_win, qs) + (bias_first if qi == 0 else bias_rest)
    probs, denoms = {}, {}
    for kh, qi in blocks:
        s = scores[kh, qi]
        m = jnp.maximum(jnp.max(s, axis=0, keepdims=True), sinks[kh])
        p = jnp.exp(s - m)
        denoms[kh, qi] = jnp.sum(p, axis=0, keepdims=True) + jnp.exp(sinks[kh] - m)
        probs[kh, qi] = p.astype(BF16)
    outs = {}
    for kh, qi in blocks:
        r0 = qi * WINDOW
        vt_win = vt_all[kh * ATT_HEAD_DIM:(kh + 1) * ATT_HEAD_DIM, r0:r0 + 2 * WINDOW]
        outs[kh, qi] = (_dot(vt_win, probs[kh, qi]) / denoms[kh, qi]).astype(BF16)
    a_t = jnp.concatenate(
        [jnp.concatenate([outs[hd // ATT_GROUP, qi][:, (hd % ATT_GROUP) * WINDOW:(hd % ATT_GROUP + 1) * WINDOW]
                          for qi in range(tm // WINDOW)], axis=1)
         for hd in range(ATT_HEADS)], axis=0)

    u_ext = jnp.concatenate([u_hist, u], axis=0)
    first_rows = (lax.broadcasted_iota(jnp.int32, (POOL_HISTORY, POOL_GROUP), 0) + 1).astype(F32)
    pooled = []
    for gi, w in enumerate(POOL_WINDOWS):
        z = u_ext[:, gi * POOL_GROUP:(gi + 1) * POOL_GROUP]
        shift = 1
        while shift < w:
            z = z + pltpu.roll(z, shift, axis=0)
            shift *= 2
        head_scale = jnp.where(j == 0, 1.0 / jnp.minimum(first_rows, float(w)), 1.0 / w)
        mean = jnp.concatenate([z[POOL_HISTORY:2 * POOL_HISTORY, :] * head_scale,
                                z[2 * POOL_HISTORY:, :] * (1.0 / w)], axis=0)
        pooled.append((mean - u[:, gi * POOL_GROUP:(gi + 1) * POOL_GROUP]).astype(BF16))
    pool_out = jnp.concatenate(
        [_dot(pooled[gi], wpool_ref[gi]) for gi in range(len(POOL_WINDOWS))], axis=1).astype(BF16)

    o_ref[...] = (x + _dot_tn(a_t, w_out_ref[:ATT_WIDTH, :])
                  + _dot(pool_out, w_out_ref[ATT_WIDTH:, :]))
    kprev_ref[...] = kr[tm - WINDOW:, :]
    vtprev_ref[...] = vt_cur[:, tm - WINDOW:]
    uhist_ref[...] = u[tm - POOL_HISTORY:, :]


def _attention_bias():
    c = np.arange(2 * WINDOW)[:, None]
    r = np.arange(ATT_GROUP * WINDOW)[None, :] % WINDOW
    band = (c > r) & (c <= r + WINDOW)
    first = band & (c >= WINDOW)
    return jnp.asarray(np.where(np.stack([band, first]), 0.0, MASKED_SCORE), F32)


def _attn_pool_mixer(x3, gain, w_in, w_out, q_gain, k_gain, sinks, pool_w, pool_scale, cos_t, sin_t):
    b, t, d = x3.shape
    tm = min(AB_ROW_TILE, t)
    ab_in = w_in.shape[1]
    head_of_lane = np.arange(KV_WIDTH) // ATT_HEAD_DIM
    bd = jnp.asarray((head_of_lane[:, None] == head_of_lane[None, :]) / ATT_HEAD_DIM, BF16)
    n_groups = len(POOL_WINDOWS)
    wpool = (pool_w * pool_scale.reshape(n_groups, 1, POOL_GROUP)).astype(BF16)
    half = ATT_HEAD_DIM // 2

    def gain_rows(g, scale):
        partner = jnp.concatenate([g[half:], g[:half]])
        return jnp.stack([jnp.tile(g, ATT_KV_HEADS), jnp.tile(partner, ATT_KV_HEADS)]) * scale

    qg = gain_rows(q_gain, ATT_HEAD_DIM ** -0.5)
    kg = gain_rows(k_gain, 1.0)
    sink_lanes = jnp.repeat(sinks, WINDOW).reshape(ATT_KV_HEADS, ATT_GROUP * WINDOW)
    return pl.pallas_call(
        _ab_kernel,
        out_shape=jax.ShapeDtypeStruct((b, t, d), F32),
        grid=(b, t // tm),
        in_specs=[
            pl.BlockSpec((None, tm, d), lambda i, j: (i, j, 0)),
            _const_spec((d, ab_in)),
            pl.BlockSpec((tm, KV_WIDTH), lambda i, j: (j, 0)),
            pl.BlockSpec((tm, KV_WIDTH), lambda i, j: (j, 0)),
            _const_spec((2, KV_WIDTH)),
            _const_spec((2, KV_WIDTH)),
            _const_spec((KV_WIDTH, KV_WIDTH)),
            _const_spec((2, 2 * WINDOW, ATT_GROUP * WINDOW)),
            _const_spec((ATT_KV_HEADS, ATT_GROUP * WINDOW)),
            _const_spec((len(POOL_WINDOWS), POOL_GROUP, POOL_GROUP)),
            _const_spec((ATT_WIDTH + POOL_WIDTH, d)),
        ],
        out_specs=pl.BlockSpec((None, tm, d), lambda i, j: (i, j, 0)),
        scratch_shapes=[
            pltpu.VMEM((WINDOW, KV_WIDTH), BF16),
            pltpu.VMEM((KV_WIDTH, WINDOW), BF16),
            pltpu.VMEM((POOL_HISTORY, POOL_WIDTH), F32),
        ],
        compiler_params=pltpu.CompilerParams(
            dimension_semantics=("arbitrary", "arbitrary"), vmem_limit_bytes=VMEM_LIMIT_BYTES),
        name="attn_pool_mixer",
    )(x3, _fold_rows(gain, w_in), cos_t, sin_t, qg, kg, bd, _attention_bias(),
      sink_lanes, wpool, w_out.astype(BF16))


def _split_bf16(a):
    hi = a.astype(BF16)
    lo = (a - hi.astype(F32)).astype(BF16)
    return hi, lo


def _node_reference(g, node, row):
    n_rows, width = g.shape
    g3 = g.reshape(n_rows // node, node, width)
    return jnp.broadcast_to(g3[:, row:row + 1, :], g3.shape).reshape(n_rows, width)


def _node_halves(a, node):
    n_rows, width = a.shape
    a3 = a.reshape(n_rows // node, node, width)
    return a3[:, :node // 2, :], a3[:, node // 2:, :]


def _hgrn_node_sizes():
    sizes, n = [], 2 * HGRN_BASE_BLOCK
    while n <= HGRN_CHUNK:
        sizes.append(n)
        n *= 2
    return sizes


def _hgrn_level_codes():
    t = np.arange(HGRN_CHUNK)[:, None]
    s = np.arange(HGRN_CHUNK)[None, :]
    code = np.full((HGRN_CHUNK, HGRN_CHUNK), 1 << 20, np.int32)
    for k, n in reversed(list(enumerate(_hgrn_node_sizes(), start=1))):
        code[(t // n) == (s // n)] = k
    same_base = (t // HGRN_BASE_BLOCK) == (s // HGRN_BASE_BLOCK)
    code[same_base] = np.where(s <= t, 0, 1 << 20)[same_base]
    return jnp.asarray(code), jnp.asarray(s <= t, BF16)


def _hgrn_kernel(layer, x_ref, w_in_ref, lb_logits_ref, w_out_ref, level_ref, tri_ref, o_ref,
                 state_ref):
    tm = x_ref.shape[0]
    dk = HGRN_DK
    cl = HGRN_CHUNK
    j = pl.program_id(1)

    @pl.when(j == 0)
    def _():
        state_ref[...] = jnp.zeros_like(state_ref)

    x = x_ref[...]
    h = _rms_normalize(x).astype(BF16)

    logits = lb_logits_ref[...]
    e = jnp.exp(logits - jnp.max(logits, axis=0, keepdims=True))
    prob = e / jnp.sum(e, axis=0, keepdims=True)
    lb = jnp.sum(prob[:layer + 1, :], axis=0, keepdims=True) - prob[0:1, :]

    tri = tri_ref[...]
    tri2 = jnp.concatenate([tri, tri], axis=1)
    level = level_ref[...]
    base = HGRN_BASE_BLOCK
    diag_mask = level == 0
    node_sizes = _hgrn_node_sizes()
    node_masks = [level <= k + 1 for k in range(len(node_sizes))]

    chunks = range(tm // cl)

    def rows(a, c):
        return a[c * cl:(c + 1) * cl]

    def stage_project(hd, _):
        return _dot(h, w_in_ref[hd])

    def stage_gates(hd, proj):
        qh = proj[:, :dk]
        lb_h = lb[:, hd * dk:(hd + 1) * dk]
        f = lb_h + (1.0 - lb_h) * _sigmoid(proj[:, dk:2 * dk])
        return dict(qf=qh * _sigmoid(qh), key=1.0 - f,
                    lf=_split_bf16(jnp.log2(jnp.maximum(f, GATE_EPS))),
                    v=proj[:, 2 * dk:3 * dk].astype(BF16), gate=_sigmoid(proj[:, 3 * dk:]))

    def stage_cumsum(hd, c, w):
        both = _dot(tri, jnp.concatenate([rows(w["lf"][0], c), rows(w["lf"][1], c)], axis=1))
        w["g", c] = both[:, :dk] + both[:, dk:]

    def stage_operands(hd, c, w):
        g, qf, key = w["g", c], rows(w["qf"], c), rows(w["key"], c)
        ref0 = _node_reference(g, base, base // 2 - 1)
        ops = [((qf * jnp.exp2(g - ref0)).astype(BF16), (key * jnp.exp2(ref0 - g)).astype(BF16))]
        for n in node_sizes:
            g_l, g_r = _node_halves(g, n)
            ref = g_l[:, n // 2 - 1:n // 2, :]
            qt_r = (_node_halves(qf, n)[1] * jnp.exp2(g_r - ref)).astype(BF16)
            kt_l = (_node_halves(key, n)[0] * jnp.exp2(ref - g_l)).astype(BF16)
            zeros = jnp.zeros_like(qt_r)
            kt = jnp.concatenate([kt_l, zeros], axis=1).reshape(cl, dk)
            if n >= HGRN_COMPACT_NODE:
                ops.append((qt_r.reshape(cl // 2, dk), kt))
            else:
                ops.append((jnp.concatenate([zeros, qt_r], axis=1).reshape(cl, dk), kt))
        g_last = g[cl - 1:cl, :]
        w["levels", c] = ops
        w["q_head", c] = (qf * jnp.exp2(g)).astype(BF16)
        w["k_tail", c] = (key * jnp.exp2(g_last - g)).astype(BF16)
        w["decay", c] = jnp.exp2(g_last)

    def stage_scores(hd, c, w):
        ops = w["levels", c]
        a = jnp.where(diag_mask, _dot_nt(*ops[0]), 0.0)
        for n, nmask, (qt, kt) in zip(node_sizes, node_masks, ops[1:]):
            lvl = _dot_nt(qt, kt)
            if n >= HGRN_COMPACT_NODE:
                pieces = []
                for node in range(cl // n):
                    right = slice(node * n + n // 2, (node + 1) * n)
                    part = lvl[node * (n // 2):(node + 1) * (n // 2)]
                    pieces += [a[node * n:node * n + n // 2],
                               a[right] + (part if n == cl else jnp.where(nmask[right], part, 0.0))]
                a = jnp.concatenate(pieces, axis=0)
            else:
                a = a + jnp.where(nmask, lvl, 0.0)
        w["scores", c] = a.astype(BF16)

    def stage_recurrence(hd, c, w):
        st = states[hd]
        vv = rows(w["v"], c)
        w["out", c] = _dot(w["scores", c], vv) + _dot_nt(w["q_head", c], st.astype(BF16))
        states[hd] = st * w["decay", c] + _dot_tn(vv, w["k_tail", c])

    def gate_and_normalize(c, work):
        gated = [w["out", c] * rows(w["gate"], c) for w in work]
        for hd, w in enumerate(work):
            og = gated[hd]
            mean_sq = jnp.mean(og * og, axis=-1, keepdims=True)
            w["y", c] = (og * lax.rsqrt(mean_sq + NORM_EPS)).astype(BF16)

    heads = range(HGRN_HEADS)
    states = [state_ref[hd] for hd in heads]
    work = [stage_gates(hd, proj) for hd, proj in enumerate([stage_project(hd, None) for hd in heads])]
    for stage in (stage_cumsum, stage_operands, stage_scores, stage_recurrence):
        for c in chunks:
            for hd in heads:
                stage(hd, c, work[hd])
    for c in chunks:
        gate_and_normalize(c, work)

    for hd in heads:
        state_ref[hd] = states[hd]
    y = jnp.concatenate(
        [jnp.concatenate([work[hd]["y", c] for c in chunks], axis=0) for hd in heads], axis=1)
    o_ref[...] = x + _dot(y, w_out_ref[...])


def _hgrn_mixer(x3, gain, w_in, w_out, out_gain, lb_logits, layer):
    b, t, d = x3.shape
    tm = min(HGRN_ROW_TILE, t)
    n_layers, kw = lb_logits.shape
    w_heads = _fold_rows(gain, w_in).reshape(d, 4, HGRN_HEADS, HGRN_DK).transpose(2, 0, 1, 3).reshape(
        HGRN_HEADS, d, 4 * HGRN_DK)
    return pl.pallas_call(
        functools.partial(_hgrn_kernel, layer),
        out_shape=jax.ShapeDtypeStruct((b, t, d), F32),
        grid=(b, t // tm),
        in_specs=[
            pl.BlockSpec((None, tm, d), lambda i, j: (i, j, 0)),
            _const_spec((HGRN_HEADS, d, 4 * HGRN_DK)),
            _const_spec((n_layers, kw)),
            _const_spec((kw, d)),
            _const_spec((HGRN_CHUNK, HGRN_CHUNK)),
            _const_spec((HGRN_CHUNK, HGRN_CHUNK)),
        ],
        out_specs=pl.BlockSpec((None, tm, d), lambda i, j: (i, j, 0)),
        scratch_shapes=[
            pltpu.VMEM((HGRN_HEADS, HGRN_DK, HGRN_DK), F32),
        ],
        compiler_params=pltpu.CompilerParams(
            dimension_semantics=("arbitrary", "arbitrary"), vmem_limit_bytes=VMEM_LIMIT_BYTES),
        name="hgrn_mixer",
    )(x3, w_heads, lb_logits, _fold_rows(jnp.tile(out_gain, HGRN_HEADS), w_out), *_hgrn_level_codes())


def kernel(x, positions, norm_gains, ffn_w_gate, ffn_w_up, ffn_w_down, ab_w_in, ab_w_out, q_norm_gain, k_norm_gain, attn_sinks, pool_w, pool_scale, c_w_in, c_w_out, c_out_norm_gain, lb_logits):
    b, t, d = x.shape
    depth = norm_gains.shape[0]
    cos_t, sin_t = _rope_tables(positions)

    ffn_gain = jnp.stack([norm_gains[:, 0], norm_gains[:, 2]], axis=1)[..., None]
    wg_all = (ffn_gain * ffn_w_gate).astype(BF16)
    wu_all = (ffn_gain * ffn_w_up).astype(BF16)
    wd_all = (0.5 * ffn_w_down).astype(BF16)

    def ffn(x3, layer, which):
        return _ffn(x3.reshape(b * t, d), wg_all, wu_all, wd_all, layer, which).reshape(b, t, d)

    for layer in range(depth):
        x = ffn(x, layer, 0)
        jx = layer // 2
        if layer % 2 == 0:
            x = _attn_pool_mixer(x, norm_gains[layer, 1], ab_w_in[jx], ab_w_out[jx], q_norm_gain[jx],
                                 k_norm_gain[jx], attn_sinks[jx], pool_w[jx], pool_scale[jx], cos_t, sin_t)
        else:
            x = _hgrn_mixer(x, norm_gains[layer, 1], c_w_in[jx], c_w_out[jx], c_out_norm_gain[jx],
                            lb_logits, jx)
        x = ffn(x, layer, 1)
    return x
```

```python
import functools

import jax
import jax.numpy as jnp
import numpy as np
from jax import lax
from jax.experimental import pallas as pl
from jax.experimental.pallas import tpu as pltpu

F32 = jnp.float32
BF16 = jnp.bfloat16

NORM_EPS = 1e-6
GATE_EPS = 1e-6
ROPE_THETA = 10000.0

ATT_HEADS = 8
ATT_KV_HEADS = 2
ATT_GROUP = ATT_HEADS // ATT_KV_HEADS
ATT_HEAD_DIM = 64
WINDOW = 128
ATT_WIDTH = ATT_HEADS * ATT_HEAD_DIM
KV_WIDTH = ATT_KV_HEADS * ATT_HEAD_DIM
POOL_WINDOWS = (2, 4, 8, 16)
POOL_GROUP = 128
POOL_WIDTH = POOL_GROUP * len(POOL_WINDOWS)
POOL_HISTORY = 16
HGRN_HEADS = 8
HGRN_DK = 128

FFN_ROW_TILE = 1024
FFN_COL_CHUNK = 256
AB_ROW_TILE = 1024
HGRN_ROW_TILE = 1024
HGRN_CHUNK = 128
HGRN_BASE_BLOCK = 8
HGRN_COMPACT_NODE = 16
VMEM_LIMIT_BYTES = 56 * 1024 * 1024
MASKED_SCORE = -1e30


def _rms_normalize(x):
    ms = jnp.mean(x * x, axis=-1, keepdims=True)
    return x * lax.rsqrt(ms + NORM_EPS)


def _fold_rows(gain, w):
    return (gain[:, None] * w).astype(BF16)


def _sigmoid(x):
    return 0.5 * jnp.tanh(0.5 * x) + 0.5


def _const_spec(shape):
    nd = len(shape)
    return pl.BlockSpec(shape, lambda *_: (0,) * nd, pipeline_mode=pl.Buffered(1))


def _dot(a, b):
    return jnp.dot(a, b, preferred_element_type=F32)


def _dot_nt(a, b):
    return lax.dot_general(a, b, (((1,), (1,)), ((), ())), preferred_element_type=F32)


def _dot_tn(a, b):
    return lax.dot_general(a, b, (((0,), (0,)), ((), ())), preferred_element_type=F32)


def _ffn_kernel(x_ref, wg_ref, wu_ref, wd_ref, o_ref):
    x = x_ref[...]
    h = _rms_normalize(x).astype(BF16)
    d_ff = wg_ref.shape[1]
    acc = x
    for c in range(d_ff // FFN_COL_CHUNK):
        sl = slice(c * FFN_COL_CHUNK, (c + 1) * FFN_COL_CHUNK)
        g = _dot(h, wg_ref[:, sl])
        u = _dot(h, wu_ref[:, sl])
        a = (g * jax.nn.sigmoid(g) * u).astype(BF16)
        acc = acc + _dot(a, wd_ref[sl, :])
    o_ref[...] = acc


def _ffn(x2, wg, wu, wd, layer, which):
    n, d = x2.shape
    d_ff = wg.shape[-1]
    tm = min(FFN_ROW_TILE, n)

    def weight_spec(rows, cols):
        return pl.BlockSpec((None, None, rows, cols), lambda i: (layer, which, 0, 0),
                            pipeline_mode=pl.Buffered(1))

    return pl.pallas_call(
        _ffn_kernel,
        out_shape=jax.ShapeDtypeStruct((n, d), F32),
        grid=(n // tm,),
        in_specs=[
            pl.BlockSpec((tm, d), lambda i: (i, 0)),
            weight_spec(d, d_ff),
            weight_spec(d, d_ff),
            weight_spec(d_ff, d),
        ],
        out_specs=pl.BlockSpec((tm, d), lambda i: (i, 0)),
        compiler_params=pltpu.CompilerParams(
            dimension_semantics=("arbitrary",), vmem_limit_bytes=VMEM_LIMIT_BYTES),
        name="ffn",
    )(x2, wg, wu, wd)


def _rope_table_kernel(pos_ref, inv_freq_ref, sign_ref, cos_ref, sin_ref):
    ang = pos_ref[...].astype(F32) * inv_freq_ref[...]
    cos_ref[...] = jnp.cos(ang)
    sin_ref[...] = jnp.sin(ang) * sign_ref[...]


def _rope_tables(positions):
    t = positions.shape[0]
    half = ATT_HEAD_DIM // 2
    lane = np.arange(KV_WIDTH)
    inv_freq = ROPE_THETA ** (-jnp.arange(half, dtype=F32) / half)
    inv_freq_lanes = jnp.tile(inv_freq, KV_WIDTH // half).reshape(1, KV_WIDTH)
    sign = jnp.asarray(np.where(lane % ATT_HEAD_DIM < half, -1.0, 1.0), F32).reshape(1, KV_WIDTH)
    tr = min(256, t)
    return pl.pallas_call(
        _rope_table_kernel,
        out_shape=(jax.ShapeDtypeStruct((t, KV_WIDTH), F32),) * 2,
        grid=(t // tr,),
        in_specs=[pl.BlockSpec((tr, 1), lambda i: (i, 0)),
                  _const_spec((1, KV_WIDTH)), _const_spec((1, KV_WIDTH))],
        out_specs=(pl.BlockSpec((tr, KV_WIDTH), lambda i: (i, 0)),) * 2,
        compiler_params=pltpu.CompilerParams(dimension_semantics=("arbitrary",)),
        name="rope_tables",
    )(positions.reshape(t, 1), inv_freq_lanes, sign)


def _rotate_half_partner(z):
    w = z.shape[1]
    lane = lax.broadcasted_iota(jnp.int32, z.shape, 1)
    first_half = (lane & (ATT_HEAD_DIM // 2)) == 0
    return jnp.where(first_half, pltpu.roll(z, w - ATT_HEAD_DIM // 2, axis=1),
                     pltpu.roll(z, ATT_HEAD_DIM // 2, axis=1))


def _ab_kernel(x_ref, w_in_ref, cos_ref, sin_ref, qg_ref, kg_ref, bd_ref, bias_ref, onehot_ref,
               sink_ref, wpool_ref, w_out_ref, o_ref,
               kprev_ref, vtprev_ref, uhist_ref):
    tm = x_ref.shape[0]
    j = pl.program_id(1)

    @pl.when(j == 0)
    def _():
        kprev_ref[...] = jnp.zeros_like(kprev_ref)
        vtprev_ref[...] = jnp.zeros_like(vtprev_ref)
        uhist_ref[...] = jnp.zeros_like(uhist_ref)

    k_prev = kprev_ref[...]
    vt_prev = vtprev_ref[...]
    u_hist = uhist_ref[...]
    bias_rest = bias_ref[0]
    bias_first = bias_ref[jnp.where(j == 0, 1, 0)]
    one_hot = onehot_ref[...]

    x = x_ref[...]
    h = _rms_normalize(x).astype(BF16)
    proj = _dot(h, w_in_ref[...])
    q = proj[:, :ATT_WIDTH]
    k = proj[:, ATT_WIDTH:ATT_WIDTH + KV_WIDTH]
    v = proj[:, ATT_WIDTH + KV_WIDTH:ATT_WIDTH + 2 * KV_WIDTH]
    u = proj[:, ATT_WIDTH + 2 * KV_WIDTH:]

    bd = bd_ref[...]
    q_sq = (q * q).astype(BF16)
    q_ms = jnp.concatenate(
        [_dot(q_sq[:, g * KV_WIDTH:(g + 1) * KV_WIDTH], bd) for g in range(ATT_WIDTH // KV_WIDTH)], axis=1)
    k_ms = _dot((k * k).astype(BF16), bd)
    cos_kv = cos_ref[...]
    sin_kv = sin_ref[...]
    groups = ATT_WIDTH // KV_WIDTH
    q_cos = jnp.concatenate([cos_kv * qg_ref[0:1, :]] * groups, axis=1)
    q_sin = jnp.concatenate([sin_kv * qg_ref[1:2, :]] * groups, axis=1)
    qr = ((q * q_cos + _rotate_half_partner(q) * q_sin) * lax.rsqrt(q_ms + NORM_EPS)).astype(BF16)
    kr = ((k * (cos_kv * kg_ref[0:1, :]) + _rotate_half_partner(k) * (sin_kv * kg_ref[1:2, :]))
          * lax.rsqrt(k_ms + NORM_EPS)).astype(BF16)
    k_all = jnp.concatenate([k_prev, kr], axis=0)
    vt_cur = v.T.astype(BF16)
    vt_all = jnp.concatenate([vt_prev, vt_cur], axis=1)

    blocks = [(kh, qi) for kh in range(ATT_KV_HEADS) for qi in range(tm // WINDOW)]
    sinks = [sink_ref[kh:kh + 1, :] for kh in range(ATT_KV_HEADS)]
    scores = {}
    for kh, qi in blocks:
        r0 = qi * WINDOW
        qs = jnp.concatenate(
            [qr[r0:r0 + WINDOW, (kh * ATT_GROUP + g) * ATT_HEAD_DIM:(kh * ATT_GROUP + g + 1) * ATT_HEAD_DIM]
             for g in range(ATT_GROUP)], axis=0)
        k_win = k_all[r0:r0 + 2 * WINDOW, kh * ATT_HEAD_DIM:(kh + 1) * ATT_HEAD_DIM]
        k_aug = jnp.concatenate([bias_first if qi == 0 else bias_rest, k_win], axis=1)
        q_aug = jnp.concatenate([one_hot, qs], axis=1)
        scores[kh, qi] = _dot_nt(k_aug, q_aug)
    probs, denoms = {}, {}
    for kh, qi in blocks:
        s = scores[kh, qi]
        m = jnp.maximum(jnp.max(s, axis=0, keepdims=True), sinks[kh])
        p = jnp.exp(s - m)
        denoms[kh, qi] = jnp.sum(p, axis=0, keepdims=True) + jnp.exp(sinks[kh] - m)
        probs[kh, qi] = p.astype(BF16)
    outs = {}
    for kh, qi in blocks:
        r0 = qi * WINDOW
        vt_win = vt_all[kh * ATT_HEAD_DIM:(kh + 1) * ATT_HEAD_DIM, r0:r0 + 2 * WINDOW]
        outs[kh, qi] = (_dot(vt_win, probs[kh, qi]) / denoms[kh, qi]).astype(BF16)
    a_t = jnp.concatenate(
        [jnp.concatenate([outs[hd // ATT_GROUP, qi][:, (hd % ATT_GROUP) * WINDOW:(hd % ATT_GROUP + 1) * WINDOW]
                          for qi in range(tm // WINDOW)], axis=1)
         for hd in range(ATT_HEADS)], axis=0)

    u_ext = jnp.concatenate([u_hist, u], axis=0)
    first_rows = (lax.broadcasted_iota(jnp.int32, (POOL_HISTORY, POOL_GROUP), 0) + 1).astype(F32)
    pooled = []
    for gi, w in enumerate(POOL_WINDOWS):
        z = u_ext[:, gi * POOL_GROUP:(gi + 1) * POOL_GROUP]
        shift = 1
        while shift < w:
            z = z + pltpu.roll(z, shift, axis=0)
            shift *= 2
        head_scale = jnp.where(j == 0, 1.0 / jnp.minimum(first_rows, float(w)), 1.0 / w)
        mean = jnp.concatenate([z[POOL_HISTORY:2 * POOL_HISTORY, :] * head_scale,
                                z[2 * POOL_HISTORY:, :] * (1.0 / w)], axis=0)
        pooled.append((mean - u[:, gi * POOL_GROUP:(gi + 1) * POOL_GROUP]).astype(BF16))
    pool_out = jnp.concatenate(
        [_dot(pooled[gi], wpool_ref[gi]) for gi in range(len(POOL_WINDOWS))], axis=1).astype(BF16)

    o_ref[...] = (x + _dot_tn(a_t, w_out_ref[:ATT_WIDTH, :])
                  + _dot(pool_out, w_out_ref[ATT_WIDTH:, :]))
    kprev_ref[...] = kr[tm - WINDOW:, :]
    vtprev_ref[...] = vt_cur[:, tm - WINDOW:]
    uhist_ref[...] = u[tm - POOL_HISTORY:, :]


def _attention_bias():
    c = np.arange(2 * WINDOW)[:, None]
    r = np.arange(WINDOW)[None, :]
    band = (c > r) & (c <= r + WINDOW)
    first = band & (c >= WINDOW)
    one_hot = np.tile(np.eye(WINDOW, dtype=np.float32), (ATT_GROUP, 1))
    return (jnp.asarray(np.where(np.stack([band, first]), 0.0, MASKED_SCORE), BF16),
            jnp.asarray(one_hot, BF16))


def _attn_pool_mixer(x3, gain, w_in, w_out, q_gain, k_gain, sinks, pool_w, pool_scale, cos_t, sin_t):
    b, t, d = x3.shape
    tm = min(AB_ROW_TILE, t)
    ab_in = w_in.shape[1]
    head_of_lane = np.arange(KV_WIDTH) // ATT_HEAD_DIM
    bd = jnp.asarray((head_of_lane[:, None] == head_of_lane[None, :]) / ATT_HEAD_DIM, BF16)
    n_groups = len(POOL_WINDOWS)
    wpool = (pool_w * pool_scale.reshape(n_groups, 1, POOL_GROUP)).astype(BF16)
    half = ATT_HEAD_DIM // 2

    def gain_rows(g, scale):
        partner = jnp.concatenate([g[half:], g[:half]])
        return jnp.stack([jnp.tile(g, ATT_KV_HEADS), jnp.tile(partner, ATT_KV_HEADS)]) * scale

    qg = gain_rows(q_gain, ATT_HEAD_DIM ** -0.5)
    kg = gain_rows(k_gain, 1.0)
    sink_lanes = jnp.repeat(sinks, WINDOW).reshape(ATT_KV_HEADS, ATT_GROUP * WINDOW)
    return pl.pallas_call(
        _ab_kernel,
        out_shape=jax.ShapeDtypeStruct((b, t, d), F32),
        grid=(b, t // tm),
        in_specs=[
            pl.BlockSpec((None, tm, d), lambda i, j: (i, j, 0)),
            _const_spec((d, ab_in)),
            pl.BlockSpec((tm, KV_WIDTH), lambda i, j: (j, 0)),
            pl.BlockSpec((tm, KV_WIDTH), lambda i, j: (j, 0)),
            _const_spec((2, KV_WIDTH)),
            _const_spec((2, KV_WIDTH)),
            _const_spec((KV_WIDTH, KV_WIDTH)),
            _const_spec((2, 2 * WINDOW, WINDOW)),
            _const_spec((ATT_GROUP * WINDOW, WINDOW)),
            _const_spec((ATT_KV_HEADS, ATT_GROUP * WINDOW)),
            _const_spec((len(POOL_WINDOWS), POOL_GROUP, POOL_GROUP)),
            _const_spec((ATT_WIDTH + POOL_WIDTH, d)),
        ],
        out_specs=pl.BlockSpec((None, tm, d), lambda i, j: (i, j, 0)),
        scratch_shapes=[
            pltpu.VMEM((WINDOW, KV_WIDTH), BF16),
            pltpu.VMEM((KV_WIDTH, WINDOW), BF16),
            pltpu.VMEM((POOL_HISTORY, POOL_WIDTH), F32),
        ],
        compiler_params=pltpu.CompilerParams(
            dimension_semantics=("arbitrary", "arbitrary"), vmem_limit_bytes=VMEM_LIMIT_BYTES),
        name="attn_pool_mixer",
    )(x3, _fold_rows(gain, w_in), cos_t, sin_t, qg, kg, bd, *_attention_bias(),
      sink_lanes, wpool, w_out.astype(BF16))


def _split_bf16(a):
    hi = a.astype(BF16)
    lo = (a - hi.astype(F32)).astype(BF16)
    return hi, lo


def _node_reference(g, node, row):
    n_rows, width = g.shape
    g3 = g.reshape(n_rows // node, node, width)
    return jnp.broadcast_to(g3[:, row:row + 1, :], g3.shape).reshape(n_rows, width)


def _node_halves(a, node):
    n_rows, width = a.shape
    a3 = a.reshape(n_rows // node, node, width)
    return a3[:, :node // 2, :], a3[:, node // 2:, :]


def _hgrn_node_sizes():
    sizes, n = [], 2 * HGRN_BASE_BLOCK
    while n <= HGRN_CHUNK:
        sizes.append(n)
        n *= 2
    return sizes


def _hgrn_level_codes():
    t = np.arange(HGRN_CHUNK)[:, None]
    s = np.arange(HGRN_CHUNK)[None, :]
    code = np.full((HGRN_CHUNK, HGRN_CHUNK), 1 << 20, np.int32)
    for k, n in reversed(list(enumerate(_hgrn_node_sizes(), start=1))):
        code[(t // n) == (s // n)] = k
    same_base = (t // HGRN_BASE_BLOCK) == (s // HGRN_BASE_BLOCK)
    code[same_base] = np.where(s <= t, 0, 1 << 20)[same_base]
    return jnp.asarray(code), jnp.asarray(s <= t, BF16)


def _hgrn_kernel(layer, x_ref, w_in_ref, lb_logits_ref, w_out_ref, level_ref, tri_ref, o_ref,
                 state_ref):
    tm = x_ref.shape[0]
    dk = HGRN_DK
    cl = HGRN_CHUNK
    j = pl.program_id(1)

    @pl.when(j == 0)
    def _():
        state_ref[...] = jnp.zeros_like(state_ref)

    x = x_ref[...]
    h = _rms_normalize(x).astype(BF16)

    logits = lb_logits_ref[...]
    e = jnp.exp(logits - jnp.max(logits, axis=0, keepdims=True))
    prob = e / jnp.sum(e, axis=0, keepdims=True)
    lb = jnp.sum(prob[:layer + 1, :], axis=0, keepdims=True) - prob[0:1, :]

    tri = tri_ref[...]
    tri2 = jnp.concatenate([tri, tri], axis=1)
    level = level_ref[...]
    base = HGRN_BASE_BLOCK
    diag_mask = level == 0
    node_sizes = _hgrn_node_sizes()
    node_masks = [level <= k + 1 for k in range(len(node_sizes))]

    chunks = range(tm // cl)

    def rows(a, c):
        return a[c * cl:(c + 1) * cl]

    def stage_project(hd, _):
        return _dot(h, w_in_ref[hd])

    def stage_gates(hd, proj):
        qh = proj[:, :dk]
        lb_h = lb[:, hd * dk:(hd + 1) * dk]
        f = lb_h + (1.0 - lb_h) * _sigmoid(proj[:, dk:2 * dk])
        return dict(qf=qh * _sigmoid(qh), key=1.0 - f,
                    lf=_split_bf16(jnp.log2(jnp.maximum(f, GATE_EPS))),
                    v=proj[:, 2 * dk:3 * dk].astype(BF16), gate=_sigmoid(proj[:, 3 * dk:]))

    def stage_cumsum(hd, c, w):
        w["g", c] = _dot(tri2, jnp.concatenate([rows(w["lf"][0], c), rows(w["lf"][1], c)], axis=0))

    def stage_operands(hd, c, w):
        g, qf, key = w["g", c], rows(w["qf"], c), rows(w["key"], c)
        ref0 = _node_reference(g, base, base // 2 - 1)
        ops = [((qf * jnp.exp2(g - ref0)).astype(BF16), (key * jnp.exp2(ref0 - g)).astype(BF16))]
        for n in node_sizes:
            g_l, g_r = _node_halves(g, n)
            ref = g_l[:, n // 2 - 1:n // 2, :]
            qt_r = (_node_halves(qf, n)[1] * jnp.exp2(g_r - ref)).astype(BF16)
            kt_l = (_node_halves(key, n)[0] * jnp.exp2(ref - g_l)).astype(BF16)
            zeros = jnp.zeros_like(qt_r)
            kt = jnp.concatenate([kt_l, zeros], axis=1).reshape(cl, dk)
            if n >= HGRN_COMPACT_NODE:
                ops.append((qt_r.reshape(cl // 2, dk), kt))
            else:
                ops.append((jnp.concatenate([zeros, qt_r], axis=1).reshape(cl, dk), kt))
        g_last = g[cl - 1:cl, :]
        w["levels", c] = ops
        w["q_head", c] = (qf * jnp.exp2(g)).astype(BF16)
        w["k_tail", c] = (key * jnp.exp2(g_last - g)).astype(BF16)
        w["decay", c] = jnp.exp2(g_last)

    def stage_scores(hd, c, w):
        ops = w["levels", c]
        a = jnp.where(diag_mask, _dot_nt(*ops[0]), 0.0)
        for n, nmask, (qt, kt) in zip(node_sizes, node_masks, ops[1:]):
            lvl = _dot_nt(qt, kt)
            if n >= HGRN_COMPACT_NODE:
                pieces = []
                for node in range(cl // n):
                    right = slice(node * n + n // 2, (node + 1) * n)
                    part = lvl[node * (n // 2):(node + 1) * (n // 2)]
                    pieces += [a[node * n:node * n + n // 2],
                               a[right] + (part if n == cl else jnp.where(nmask[right], part, 0.0))]
                a = jnp.concatenate(pieces, axis=0)
            else:
                a = a + jnp.where(nmask, lvl, 0.0)
        w["scores", c] = a.astype(BF16)

    def stage_recurrence(hd, c, w):
        st = states[hd]
        vv = rows(w["v"], c)
        w["out", c] = _dot(w["scores", c], vv) + _dot_nt(w["q_head", c], st.astype(BF16))
        states[hd] = st * w["decay", c] + _dot_tn(vv, w["k_tail", c])

    def gate_and_normalize(c, work):
        gated = [w["out", c] * rows(w["gate"], c) for w in work]
        for hd, w in enumerate(work):
            og = gated[hd]
            mean_sq = jnp.mean(og * og, axis=-1, keepdims=True)
            w["y", c] = (og * lax.rsqrt(mean_sq + NORM_EPS)).astype(BF16)

    heads = range(HGRN_HEADS)
    states = [state_ref[hd] for hd in heads]
    work = [stage_gates(hd, proj) for hd, proj in enumerate([stage_project(hd, None) for hd in heads])]
    for stage in (stage_cumsum, stage_operands, stage_scores, stage_recurrence):
        for c in chunks:
            for hd in heads:
                stage(hd, c, work[hd])
    for c in chunks:
        gate_and_normalize(c, work)

    for hd in heads:
        state_ref[hd] = states[hd]
    y = jnp.concatenate(
        [jnp.concatenate([work[hd]["y", c] for c in chunks], axis=0) for hd in heads], axis=1)
    o_ref[...] = x + _dot(y, w_out_ref[...])


def _hgrn_mixer(x3, gain, w_in, w_out, out_gain, lb_logits, layer):
    b, t, d = x3.shape
    tm = min(HGRN_ROW_TILE, t)
    n_layers, kw = lb_logits.shape
    w_heads = _fold_rows(gain, w_in).reshape(d, 4, HGRN_HEADS, HGRN_DK).transpose(2, 0, 1, 3).reshape(
        HGRN_HEADS, d, 4 * HGRN_DK)
    return pl.pallas_call(
        functools.partial(_hgrn_kernel, layer),
        out_shape=jax.ShapeDtypeStruct((b, t, d), F32),
        grid=(b, t // tm),
        in_specs=[
            pl.BlockSpec((None, tm, d), lambda i, j: (i, j, 0)),
            _const_spec((HGRN_HEADS, d, 4 * HGRN_DK)),
            _const_spec((n_layers, kw)),
            _const_spec((kw, d)),
            _const_spec((HGRN_CHUNK, HGRN_CHUNK)),
            _const_spec((HGRN_CHUNK, HGRN_CHUNK)),
        ],
        out_specs=pl.BlockSpec((None, tm, d), lambda i, j: (i, j, 0)),
        scratch_shapes=[
            pltpu.VMEM((HGRN_HEADS, HGRN_DK, HGRN_DK), F32),
        ],
        compiler_params=pltpu.CompilerParams(
            dimension_semantics=("arbitrary", "arbitrary"), vmem_limit_bytes=VMEM_LIMIT_BYTES),
        name="hgrn_mixer",
    )(x3, w_heads, lb_logits, _fold_rows(jnp.tile(out_gain, HGRN_HEADS), w_out), *_hgrn_level_codes())


def kernel(x, positions, norm_gains, ffn_w_gate, ffn_w_up, ffn_w_down, ab_w_in, ab_w_out, q_norm_gain, k_norm_gain, attn_sinks, pool_w, pool_scale, c_w_in, c_w_out, c_out_norm_gain, lb_logits):
    b, t, d = x.shape
    depth = norm_gains.shape[0]
    cos_t, sin_t = _rope_tables(positions)

    ffn_gain = jnp.stack([norm_gains[:, 0], norm_gains[:, 2]], axis=1)[..., None]
    wg_all = (ffn_gain * ffn_w_gate).astype(BF16)
    wu_all = (ffn_gain * ffn_w_up).astype(BF16)
    wd_all = (0.5 * ffn_w_down).astype(BF16)

    def ffn(x3, layer, which):
        return _ffn(x3.reshape(b * t, d), wg_all, wu_all, wd_all, layer, which).reshape(b, t, d)

    for layer in range(depth):
        x = ffn(x, layer, 0)
        jx = layer // 2
        if layer % 2 == 0:
            x = _attn_pool_mixer(x, norm_gains[layer, 1], ab_w_in[jx], ab_w_out[jx], q_norm_gain[jx],
                                 k_norm_gain[jx], attn_sinks[jx], pool_w[jx], pool_scale[jx], cos_t, sin_t)
        else:
            x = _hgrn_mixer(x, norm_gains[layer, 1], c_w_in[jx], c_w_out[jx], c_out_norm_gain[jx],
                            lb_logits, jx)
        x = ffn(x, layer, 1)
    return x
```
